```python
import jax
import jax.numpy as jnp
from jax import lax
import numpy as np

D_MODEL = 1024
BATCH = 32
SEQ = 2048
DEPTH = 2

F32 = jnp.float32
NORM_EPS = 1e-6
PLE_DIM = 256
A_HEADS = 8
A_KV_HEADS = 2
HEAD_DIM = 64
ROPE_DIM = HEAD_DIM // 4
ROPE_THETA = 500000.0
IDX_HEADS = 4
IDX_DIM = 64
TOPK_MAX = 256
Q_BLOCK = 128
LRU_WIDTH = 512
LRU_BLOCKS = 8
CONV_W = 4
LRU_C = 8.0
EVEN_IN = A_HEADS * HEAD_DIM + 2 * A_KV_HEADS * HEAD_DIM + IDX_HEADS * IDX_DIM + IDX_DIM + IDX_HEADS + 2 * LRU_WIDTH
EVEN_MIX = A_HEADS * HEAD_DIM + LRU_WIDTH
SSD_INNER = 2 * D_MODEL
SSD_HEAD_DIM = 64
SSD_HEADS = SSD_INNER // SSD_HEAD_DIM
SSD_GROUPS = 4
SSD_STATE = 128
SSD_CHUNK = 128
SSD_CONV_CH = SSD_INNER + 2 * SSD_GROUPS * SSD_STATE
ODD_IN = SSD_INNER + SSD_CONV_CH + SSD_HEADS
D_FF = 3584
N_EXPERTS = 8
TOP_K = 2

kernel_name = 'hybrid_dsa_rglru_ssd_moe_ple'


def _split(u, widths):
    out, start = [], 0
    for w in widths:
        out.append(u[..., start:start + w])
        start += w
    return out


def rmsnorm(x, g):
    xf = x.astype(F32)
    y = xf * lax.rsqrt(jnp.mean(xf * xf, axis=-1, keepdims=True) + NORM_EPS)
    return (y * g.astype(F32)).astype(x.dtype)


def partial_rope(x, pos):
    half = ROPE_DIM // 2
    inv = ROPE_THETA ** (-jnp.arange(half, dtype=F32) / half)
    ang = pos.astype(F32)[:, None] * inv[None, :]
    cos = jnp.cos(ang)[None, :, None, :]
    sin = jnp.sin(ang)[None, :, None, :]
    xr = x[..., :ROPE_DIM].astype(F32)
    x1, x2 = xr[..., :half], xr[..., half:]
    rot = jnp.concatenate([x1 * cos - x2 * sin, x1 * sin + x2 * cos], axis=-1).astype(x.dtype)
    return jnp.concatenate([rot, x[..., ROPE_DIM:]], axis=-1)


def causal_dwconv(x, w, b):
    c = x.shape[-1]
    y = lax.conv_general_dilated(x, w[:, None, :], window_strides=(1,), padding=[(CONV_W - 1, 0)],
                                 dimension_numbers=('NWC', 'WIO', 'NWC'), feature_group_count=c)
    return y + b


def swiglu(h, wg, wu, wd):
    return (jax.nn.silu(h @ wg) * (h @ wu)) @ wd


def dsa_mixer(q, k, v, qi, ki, wi):
    bsz, seq = q.shape[0], q.shape[1]
    n_sel = min(TOPK_MAX, seq // 4)
    n_blk = seq // Q_BLOCK
    groups = A_HEADS // A_KV_HEADS
    kpos = jnp.arange(seq)
    wi = wi.astype(F32) * IDX_HEADS ** -0.5

    def block(j):
        t0 = j * Q_BLOCK
        qb = lax.dynamic_slice_in_dim(q, t0, Q_BLOCK, axis=1)
        qib = lax.dynamic_slice_in_dim(qi, t0, Q_BLOCK, axis=1)
        wib = lax.dynamic_slice_in_dim(wi, t0, Q_BLOCK, axis=1)
        tpos = t0 + jnp.arange(Q_BLOCK)
        s_idx = jnp.einsum('bthd,bsd->bths', qib, ki, preferred_element_type=F32) * IDX_DIM ** -0.5
        score = jnp.einsum('bth,bths->bts', wib, jax.nn.relu(s_idx))
        causal = kpos[None, :] <= tpos[:, None]
        score = jnp.where(causal[None], score, -jnp.inf)
        _, sel = lax.top_k(score, n_sel)
        valid = sel <= tpos[None, :, None]
        kg = jax.vmap(lambda kk, ii: kk[ii])(k, sel)
        vg = jax.vmap(lambda vv, ii: vv[ii])(v, sel)
        qg = qb.reshape(bsz, Q_BLOCK, A_KV_HEADS, groups, HEAD_DIM)
        logits = jnp.einsum('btkgd,btskd->btkgs', qg, kg, preferred_element_type=F32) * HEAD_DIM ** -0.5
        logits = jnp.where(valid[:, :, None, None, :], logits, -jnp.inf)
        probs = jax.nn.softmax(logits, axis=-1).astype(v.dtype)
        o = jnp.einsum('btkgs,btskd->btkgd', probs, vg)
        return o.reshape(bsz, Q_BLOCK, A_HEADS * HEAD_DIM)

    out = lax.map(block, jnp.arange(n_blk))
    return jnp.moveaxis(out, 0, 1).reshape(bsz, seq, A_HEADS * HEAD_DIM)


def rglru_mixer(xg, xr, conv_w, conv_b, wa, ba, wx, bx, lam):
    bsz, seq, width = xr.shape
    xc = causal_dwconv(xr, conv_w, conv_b)
    xb = xc.reshape(bsz, seq, LRU_BLOCKS, width // LRU_BLOCKS)
    gate_r = jax.nn.sigmoid(jnp.einsum('bsnc,ncd->bsnd', xb, wa, preferred_element_type=F32).reshape(bsz, seq, width) + ba.astype(F32))
    gate_i = jax.nn.sigmoid(jnp.einsum('bsnc,ncd->bsnd', xb, wx, preferred_element_type=F32).reshape(bsz, seq, width) + bx.astype(F32))
    log_a = -LRU_C * gate_r * jax.nn.softplus(-lam.astype(F32))
    a = jnp.exp(log_a)
    b_in = jnp.sqrt(-jnp.expm1(2.0 * log_a)) * (gate_i * xc.astype(F32))

    def combine(lhs, rhs):
        a1, b1 = lhs
        a2, b2 = rhs
        return a1 * a2, a2 * b1 + b2

    _, hseq = lax.associative_scan(combine, (a, b_in), axis=1)
    y = hseq * jax.nn.gelu(xg.astype(F32), approximate=True)
    return y.astype(xr.dtype)


def ssd_chunked(x, dt, a_neg, bm, cm):
    bsz, seq, nh, hp = x.shape
    ng, ns = bm.shape[2], bm.shape[3]
    hg = nh // ng
    nc, cl = seq // SSD_CHUNK, SSD_CHUNK

    def chunks(t):
        return jnp.moveaxis(t.reshape((bsz, nc, cl) + t.shape[2:]), 1, 0)

    xdt = (x.astype(F32) * dt[..., None]).reshape(bsz, seq, ng, hg, hp)
    da = (dt * a_neg).reshape(bsz, seq, ng, hg)
    xs = (chunks(xdt), chunks(da), chunks(bm.astype(F32)), chunks(cm.astype(F32)))
    causal = jnp.tril(jnp.ones((cl, cl), dtype=bool))

    def step(state, inp):
        xc, ac, bc, cc = inp
        cum = jnp.cumsum(ac, axis=1)
        seg = cum[:, :, None] - cum[:, None, :]
        decay = jnp.exp(jnp.where(causal[None, :, :, None, None], seg, -jnp.inf))
        cb = jnp.einsum('btgn,bsgn->btsg', cc, bc)
        y_diag = jnp.einsum('btsg,btsgh,bsghp->btghp', cb, decay, xc)
        y_off = jnp.einsum('btgn,bghpn,btgh->btghp', cc, state, jnp.exp(cum))
        tail = jnp.exp(cum[:, -1:] - cum)
        new_state = state * jnp.exp(cum[:, -1])[..., None, None] + jnp.einsum('bsgn,bsgh,bsghp->bghpn', bc, tail, xc)
        return new_state, y_diag + y_off

    state0 = jnp.zeros((bsz, ng, hg, hp, ns), F32)
    _, ys = lax.scan(step, state0, xs)
    return jnp.moveaxis(ys, 0, 1).reshape(bsz, seq, nh, hp)


def ssd_mixer(hn, w_in, conv_w, conv_b, dt_bias, a_log, d_skip, gnorm, w_out):
    bsz, seq, _ = hn.shape
    z, xbc, dt = _split(hn @ w_in, (SSD_INNER, SSD_CONV_CH, SSD_HEADS))
    xbc = jax.nn.silu(causal_dwconv(xbc, conv_w, conv_b))
    xs, bm, cm = _split(xbc, (SSD_INNER, SSD_GROUPS * SSD_STATE, SSD_GROUPS * SSD_STATE))
    dt = jax.nn.softplus(dt.astype(F32) + dt_bias.astype(F32))
    a_neg = -jnp.exp(a_log.astype(F32))
    xh = xs.reshape(bsz, seq, SSD_HEADS, SSD_HEAD_DIM)
    y = ssd_chunked(xh, dt, a_neg,
                    bm.reshape(bsz, seq, SSD_GROUPS, SSD_STATE),
                    cm.reshape(bsz, seq, SSD_GROUPS, SSD_STATE))
    y = y + d_skip.astype(F32)[:, None] * xh.astype(F32)
    y = y.reshape(bsz, seq, SSD_INNER) * jax.nn.silu(z.astype(F32))
    yg = y.reshape(bsz, seq, SSD_GROUPS, SSD_INNER // SSD_GROUPS)
    yg = yg * lax.rsqrt(jnp.mean(yg * yg, axis=-1, keepdims=True) + NORM_EPS)
    y = (yg.reshape(bsz, seq, SSD_INNER) * gnorm.astype(F32)).astype(hn.dtype)
    return y @ w_out


def moe_swiglu(hn, router, wg, wu, wd):
    logits = jnp.einsum('bsd,de->bse', hn, router, preferred_element_type=F32)
    top_val, top_idx = lax.top_k(logits, TOP_K)
    top_w = jax.nn.softmax(top_val, axis=-1)
    gates = jnp.sum(jax.nn.one_hot(top_idx, N_EXPERTS, dtype=F32) * top_w[..., None], axis=-2)
    out = jnp.zeros(hn.shape, F32)
    for e in range(N_EXPERTS):
        out = out + gates[..., e:e + 1] * swiglu(hn, wg[e], wu[e], wd[e]).astype(F32)
    return out.astype(hn.dtype)


def ple_add(h, p_i, proj, gate_norm, gate_w):
    emb = (p_i @ proj).astype(F32)
    g = jax.nn.sigmoid((rmsnorm(h, gate_norm) @ gate_w).astype(F32))
    return h + (g * emb).astype(h.dtype)


def even_layer(h, p_i, mix_norm, w_in, lru_conv_w, lru_conv_b, lru_wa, lru_ba, lru_wx, lru_bx, lru_lambda,
               w_out, ffn_norm, ffn_wg, ffn_wu, ffn_wd, ple_proj, ple_gate_norm, ple_gate):
    bsz, seq, _ = h.shape
    pos = jnp.arange(seq)
    u = rmsnorm(h, mix_norm) @ w_in
    q, k, v, qi, ki, wi, xg, xr = _split(u, (A_HEADS * HEAD_DIM, A_KV_HEADS * HEAD_DIM, A_KV_HEADS * HEAD_DIM,
                                             IDX_HEADS * IDX_DIM, IDX_DIM, IDX_HEADS, LRU_WIDTH, LRU_WIDTH))
    q = partial_rope(q.reshape(bsz, seq, A_HEADS, HEAD_DIM), pos)
    k = partial_rope(k.reshape(bsz, seq, A_KV_HEADS, HEAD_DIM), pos)
    v = v.reshape(bsz, seq, A_KV_HEADS, HEAD_DIM)
    qi = partial_rope(qi.reshape(bsz, seq, IDX_HEADS, IDX_DIM), pos)
    ki = partial_rope(ki[:, :, None, :], pos)[:, :, 0, :]
    o_a = dsa_mixer(q, k, v, qi, ki, wi)
    o_b = rglru_mixer(xg, xr, lru_conv_w, lru_conv_b, lru_wa, lru_ba, lru_wx, lru_bx, lru_lambda)
    h = h + jnp.concatenate([o_a, o_b], axis=-1) @ w_out
    h = h + swiglu(rmsnorm(h, ffn_norm), ffn_wg, ffn_wu, ffn_wd)
    return ple_add(h, p_i, ple_proj, ple_gate_norm, ple_gate)


def odd_layer(h, p_i, mix_norm, w_in, conv_w, conv_b, dt_bias, a_log, d_skip, gnorm, w_out,
              moe_norm, router, exp_wg, exp_wu, exp_wd, ple_proj, ple_gate_norm, ple_gate):
    h = h + ssd_mixer(rmsnorm(h, mix_norm), w_in, conv_w, conv_b, dt_bias, a_log, d_skip, gnorm, w_out)
    h = h + moe_swiglu(rmsnorm(h, moe_norm), router, exp_wg, exp_wu, exp_wd)
    return ple_add(h, p_i, ple_proj, ple_gate_norm, ple_gate)


def setup_inputs(seed: int = 0) -> dict:
    key = jax.random.key(seed)
    keys = iter(jax.random.split(key, 64))

    def nk():
        return next(keys)

    def dense(shape, fan_in):
        return jax.random.normal(nk(), shape, F32) * fan_in ** -0.5

    def gain(n):
        return 1.0 + 0.05 * jax.random.normal(nk(), (n,), F32)

    def bias(n):
        return 0.02 * jax.random.normal(nk(), (n,), F32)

    x = jax.random.normal(nk(), (BATCH, SEQ, D_MODEL), F32)
    p = jax.random.normal(nk(), (DEPTH, BATCH, SEQ, PLE_DIM), F32)
    blk = LRU_WIDTH // LRU_BLOCKS
    u = jax.random.uniform(nk(), (LRU_WIDTH,), F32, minval=0.9, maxval=0.999)
    s = u ** (1.0 / LRU_C)
    lru_lambda = jnp.log(s) - jnp.log1p(-s)
    dt0 = jnp.exp(jax.random.uniform(nk(), (SSD_HEADS,), F32, minval=np.log(1e-3), maxval=np.log(1e-1)))
    dt_bias = dt0 + jnp.log(-jnp.expm1(-dt0))
    a_log = jnp.log(jax.random.uniform(nk(), (SSD_HEADS,), F32, minval=1.0, maxval=16.0))
    return {
        'x': x,
        'p': p,
        'e_mix_norm': gain(D_MODEL),
        'e_w_in': dense((D_MODEL, EVEN_IN), D_MODEL),
        'e_lru_conv_w': dense((CONV_W, LRU_WIDTH), CONV_W),
        'e_lru_conv_b': bias(LRU_WIDTH),
        'e_lru_wa': dense((LRU_BLOCKS, blk, blk), blk),
        'e_lru_ba': bias(LRU_WIDTH),
        'e_lru_wx': dense((LRU_BLOCKS, blk, blk), blk),
        'e_lru_bx': bias(LRU_WIDTH),
        'e_lru_lambda': lru_lambda,
        'e_w_out': dense((EVEN_MIX, D_MODEL), EVEN_MIX),
        'e_ffn_norm': gain(D_MODEL),
        'e_ffn_wg': dense((D_MODEL, D_FF), D_MODEL),
        'e_ffn_wu': dense((D_MODEL, D_FF), D_MODEL),
        'e_ffn_wd': dense((D_FF, D_MODEL), D_FF),
        'e_ple_proj': dense((PLE_DIM, D_MODEL), PLE_DIM),
        'e_ple_gate_norm': gain(D_MODEL),
        'e_ple_gate': dense((D_MODEL, D_MODEL), D_MODEL),
        'o_mix_norm': gain(D_MODEL),
        'o_w_in': dense((D_MODEL, ODD_IN), D_MODEL),
        'o_conv_w': dense((CONV_W, SSD_CONV_CH), CONV_W),
        'o_conv_b': bias(SSD_CONV_CH),
        'o_dt_bias': dt_bias,
        'o_a_log': a_log,
        'o_d_skip': gain(SSD_HEADS),
        'o_gnorm': gain(SSD_INNER),
        'o_w_out': dense((SSD_INNER, D_MODEL), SSD_INNER),
        'o_moe_norm': gain(D_MODEL),
        'o_router': dense((D_MODEL, N_EXPERTS), D_MODEL),
        'o_exp_wg': dense((N_EXPERTS, D_MODEL, D_FF), D_MODEL),
        'o_exp_wu': dense((N_EXPERTS, D_MODEL, D_FF), D_MODEL),
        'o_exp_wd': dense((N_EXPERTS, D_FF, D_MODEL), D_FF),
        'o_ple_proj': dense((PLE_DIM, D_MODEL), PLE_DIM),
        'o_ple_gate_norm': gain(D_MODEL),
        'o_ple_gate': dense((D_MODEL, D_MODEL), D_MODEL),
        'final_norm': gain(D_MODEL),
    }


def reference(x, p, e_mix_norm, e_w_in, e_lru_conv_w, e_lru_conv_b, e_lru_wa, e_lru_ba, e_lru_wx, e_lru_bx,
              e_lru_lambda, e_w_out, e_ffn_norm, e_ffn_wg, e_ffn_wu, e_ffn_wd, e_ple_proj, e_ple_gate_norm,
              e_ple_gate, o_mix_norm, o_w_in, o_conv_w, o_conv_b, o_dt_bias, o_a_log, o_d_skip, o_gnorm, o_w_out,
              o_moe_norm, o_router, o_exp_wg, o_exp_wu, o_exp_wd, o_ple_proj, o_ple_gate_norm, o_ple_gate,
              final_norm):
    h = x
    for i in range(DEPTH):
        if i % 2 == 0:
            h = even_layer(h, p[i], e_mix_norm, e_w_in, e_lru_conv_w, e_lru_conv_b, e_lru_wa, e_lru_ba, e_lru_wx,
                           e_lru_bx, e_lru_lambda, e_w_out, e_ffn_norm, e_ffn_wg, e_ffn_wu, e_ffn_wd,
                           e_ple_proj, e_ple_gate_norm, e_ple_gate)
        else:
            h = odd_layer(h, p[i], o_mix_norm, o_w_in, o_conv_w, o_conv_b, o_dt_bias, o_a_log, o_d_skip, o_gnorm,
                          o_w_out, o_moe_norm, o_router, o_exp_wg, o_exp_wu, o_exp_wd,
                          o_ple_proj, o_ple_gate_norm, o_ple_gate)
    return rmsnorm(h, final_norm)
```

```python
import functools
import math

import jax
import jax.numpy as jnp
from jax import lax
from jax.experimental import pallas as pl
from jax.experimental.pallas import tpu as pltpu

F32 = jnp.float32
BF16 = jnp.bfloat16

V7X_LANES = 128
V7X_SUBLANES = 8
V7X_VMEM_LIMIT_BYTES = 56 * 1024 * 1024

NORM_EPS = 1e-6
HEAD_DIM = 64
A_HEADS = 8
A_KV_HEADS = 2
ROPE_DIM = HEAD_DIM // 4
ROPE_THETA = 500000.0
IDX_HEADS = 4
IDX_DIM = 64
TOPK_MAX = 256
Q_BLOCK = 128
LRU_WIDTH = 512
LRU_BLOCKS = 8
CONV_W = 4
LRU_C = 8.0
SSD_HEAD_DIM = 64
SSD_GROUPS = 4
SSD_STATE = 128
SSD_CHUNK = 128
N_EXPERTS = 8
F32_MAX = float(jnp.finfo(jnp.float32).max)


def _params(*semantics):
    return pltpu.CompilerParams(dimension_semantics=semantics,
                                vmem_limit_bytes=V7X_VMEM_LIMIT_BYTES)


def _rms(x, g):
    ms = jnp.mean(x * x, axis=-1, keepdims=True)
    return x * lax.rsqrt(ms + NORM_EPS) * g


def _silu(x):
    return x * jax.nn.sigmoid(x)


def _softplus(x):
    return jnp.maximum(x, 0.0) + jnp.log1p(jnp.exp(-jnp.abs(x)))


def _row_shift(v, d, fill, row):
    return jnp.where(row >= d, pltpu.roll(v, d, 0), fill)


def _in_proj_kernel(*refs, sections, use_rope):
    if use_rope:
        x_ref, g_ref, w_ref, cos_ref, sa_ref, sb_ref = refs[:6]
        out_refs = refs[6:]
    else:
        x_ref, g_ref, w_ref = refs[:3]
        out_refs = refs[3:]
    xn = _rms(x_ref[...], g_ref[...]).astype(BF16)
    off = 0
    for out_ref, (width, rope) in zip(out_refs, sections):
        y = jnp.dot(xn, w_ref[:, off:off + width], preferred_element_type=F32)
        if rope:
            c, sa, sb = cos_ref[...], sa_ref[...], sb_ref[...]
            for ch in range(width // V7X_LANES):
                lo = ch * V7X_LANES
                yc = y[:, lo:lo + V7X_LANES]
                yr = (yc * c + pltpu.roll(yc, V7X_LANES - ROPE_DIM // 2, 1) * sa
                      + pltpu.roll(yc, ROPE_DIM // 2, 1) * sb)
                out_ref[:, lo:lo + V7X_LANES] = yr.astype(out_ref.dtype)
        else:
            out_ref[...] = y.astype(out_ref.dtype)
        off += width


def _in_proj(x, gain, w, sections, dtypes, tm, rope_tables=None, seq=None):
    m, d = x.shape
    n = w.shape[1]
    use_rope = rope_tables is not None
    in_specs = [pl.BlockSpec((tm, d), lambda i: (i, 0)),
                pl.BlockSpec((1, d), lambda i: (0, 0)),
                pl.BlockSpec((d, n), lambda i: (0, 0))]
    args = [x, gain, w]
    if use_rope:
        nper = seq // tm
        for t in rope_tables:
            in_specs.append(pl.BlockSpec((tm, V7X_LANES), lambda i: (i % nper, 0)))
            args.append(t)
    out_shape = [jax.ShapeDtypeStruct((m, wd), dt) for (wd, _), dt in zip(sections, dtypes)]
    out_specs = [pl.BlockSpec((tm, wd), lambda i: (i, 0)) for (wd, _) in sections]
    return pl.pallas_call(
        functools.partial(_in_proj_kernel, sections=sections, use_rope=use_rope),
        grid=(m // tm,), in_specs=in_specs, out_specs=out_specs, out_shape=out_shape,
        compiler_params=_params("parallel"), name="in_proj")(*args)


def _rope_tables(seq):
    half = ROPE_DIM // 2
    inv = ROPE_THETA ** (-jnp.arange(half, dtype=F32) / half)
    ang = jnp.arange(seq, dtype=F32)[:, None] * inv[None, :]
    cos, sin = jnp.cos(ang), jnp.sin(ang)
    ones = jnp.ones((seq, HEAD_DIM - ROPE_DIM), F32)
    zeros = jnp.zeros((seq, HEAD_DIM - ROPE_DIM), F32)
    zh = jnp.zeros((seq, half), F32)
    c = jnp.concatenate([cos, cos, ones], axis=1)
    sa = jnp.concatenate([-sin, zh, zeros], axis=1)
    sb = jnp.concatenate([zh, sin, zeros], axis=1)
    rep = V7X_LANES // HEAD_DIM
    return tuple(jnp.tile(t, (1, rep)) for t in (c, sa, sb))


def _dsa_body(q_ref, qi_ref, wi_ref, kk_ref, vv_ref, kiki_ref, o_ref, score_ref, bias_ref,
              *, sk, n_sel):
    t = Q_BLOCK
    j = pl.program_id(1)
    lane = lax.broadcasted_iota(jnp.int32, (t, V7X_LANES), 1)
    lo_half = lane < HEAD_DIM
    row = j * t + lax.broadcasted_iota(jnp.int32, (t, 1), 0)
    col = lax.broadcasted_iota(jnp.int32, (t, sk), 1)
    nt = (((1,), (1,)), ((), ()))

    kiki = kiki_ref[0:sk, :]
    wi = wi_ref[...] * (IDX_HEADS ** -0.5 * IDX_DIM ** -0.5)
    score = jnp.zeros((t, sk), F32)
    for h in range(IDX_HEADS):
        c = h // 2
        qc = qi_ref[:, c * V7X_LANES:(c + 1) * V7X_LANES]
        qm = jnp.where(lo_half if h % 2 == 0 else jnp.logical_not(lo_half), qc, jnp.zeros_like(qc))
        s = lax.dot_general(qm, kiki, nt, preferred_element_type=F32)
        score = score + wi[:, h:h + 1] * jnp.maximum(s, 0.0)
    score_ref[:, 0:sk] = jnp.where(col <= row, score, -jnp.inf)

    def search(i, u):
        u_try = u | lax.shift_left(jnp.int32(1), 31 - i)
        bits = jnp.where(u_try < 0, u_try ^ jnp.int32(-2 ** 31), ~u_try)
        cand = lax.bitcast_convert_type(bits, F32)
        cnt = jnp.sum(jnp.where(score_ref[:, 0:sk] >= cand, 1.0, 0.0), axis=1, keepdims=True)
        return jnp.where(cnt >= n_sel, u_try, u)

    u = lax.fori_loop(0, 32, search, jnp.zeros((t, 1), jnp.int32))
    bits = jnp.where(u < 0, u ^ jnp.int32(-2 ** 31), ~u)
    thr = lax.bitcast_convert_type(bits, F32)
    thr = jnp.where(row + 1 <= n_sel, -F32_MAX, thr)

    cnt_gt = jnp.sum(jnp.where(score_ref[:, 0:sk] > thr, 1.0, 0.0), axis=1, keepdims=True)
    need = n_sel - cnt_gt
    tri = (lax.broadcasted_iota(jnp.int32, (V7X_LANES, V7X_LANES), 0)
           <= lax.broadcasted_iota(jnp.int32, (V7X_LANES, V7X_LANES), 1))
    tri = jnp.where(tri, 1.0, 0.0).astype(BF16)
    carry = jnp.zeros((t, 1), F32)
    for c in range(sk // V7X_LANES):
        sc = score_ref[:, c * V7X_LANES:(c + 1) * V7X_LANES]
        eq = sc == thr
        pre = jnp.dot(jnp.where(eq, 1.0, 0.0).astype(BF16), tri, preferred_element_type=F32) + carry
        carry = pre[:, V7X_LANES - 1:V7X_LANES]
        tie = jnp.where(eq, jnp.where(pre <= need, 0.0, -jnp.inf), -jnp.inf)
        bias_ref[:, c * V7X_LANES:(c + 1) * V7X_LANES] = jnp.where(sc > thr, 0.0, tie)

    scale = HEAD_DIM ** -0.5
    heads_per_chunk = V7X_LANES // HEAD_DIM
    groups = A_HEADS // A_KV_HEADS
    for c in range(A_HEADS // heads_per_chunk):
        g = (c * heads_per_chunk) // groups
        kk = kk_ref[0:sk, g * V7X_LANES:(g + 1) * V7X_LANES]
        vv = vv_ref[0:sk, g * V7X_LANES:(g + 1) * V7X_LANES]
        qc = q_ref[:, c * V7X_LANES:(c + 1) * V7X_LANES]
        outs = []
        for half in range(heads_per_chunk):
            qm = jnp.where(lo_half if half == 0 else jnp.logical_not(lo_half), qc, jnp.zeros_like(qc))
            lg = lax.dot_general(qm, kk, nt, preferred_element_type=F32) * scale + bias_ref[:, 0:sk]
            mx = jnp.max(lg, axis=-1, keepdims=True)
            p = jnp.exp(lg - mx)
            den = jnp.sum(p, axis=-1, keepdims=True)
            outs.append(jnp.dot(p.astype(BF16), vv, preferred_element_type=F32) / den)
        o_ref[:, c * V7X_LANES:(c + 1) * V7X_LANES] = jnp.where(lo_half, outs[0], outs[1]).astype(o_ref.dtype)


def _dsa_kernel(q_ref, qi_ref, wi_ref, kk_ref, vv_ref, kiki_ref, o_ref, score_ref, bias_ref,
                *, classes, n_sel):
    j = pl.program_id(1)
    prev = 0
    for jmax, sk in classes:
        @pl.when(jnp.logical_and(j >= prev, j < jmax))
        def _(sk=sk):
            _dsa_body(q_ref, qi_ref, wi_ref, kk_ref, vv_ref, kiki_ref, o_ref, score_ref, bias_ref,
                      sk=sk, n_sel=n_sel)
        prev = jmax


def _dsa_classes(nblk):
    n_cls = min(4, nblk)
    bounds = sorted({-(-nblk * (i + 1) // n_cls) for i in range(n_cls)})
    return tuple((b, b * Q_BLOCK) for b in bounds)


def _dsa(q, qi, wi, kk, vv, kiki, bsz, seq):
    n_sel = min(TOPK_MAX, seq // 4)
    nblk = seq // Q_BLOCK
    t = Q_BLOCK

    def qmap(b, j):
        return (b * nblk + j, 0)

    def kmap(b, j):
        return (b, 0)

    return pl.pallas_call(
        functools.partial(_dsa_kernel, classes=_dsa_classes(nblk), n_sel=float(n_sel)),
        grid=(bsz, nblk),
        in_specs=[pl.BlockSpec((t, q.shape[1]), qmap),
                  pl.BlockSpec((t, qi.shape[1]), qmap),
                  pl.BlockSpec((t, wi.shape[1]), qmap),
                  pl.BlockSpec((seq, kk.shape[1]), kmap),
                  pl.BlockSpec((seq, vv.shape[1]), kmap),
                  pl.BlockSpec((seq, kiki.shape[1]), kmap)],
        out_specs=pl.BlockSpec((t, q.shape[1]), qmap),
        out_shape=jax.ShapeDtypeStruct(q.shape, BF16),
        scratch_shapes=[pltpu.VMEM((t, seq), F32), pltpu.VMEM((t, seq), F32)],
        compiler_params=_params("parallel", "arbitrary"), name="dsa")(q, qi, wi, kk, vv, kiki)


def _lru_kernel(xr_ref, xg_ref, cw_ref, cb_ref, wa_ref, ba_ref, wx_ref, bx_ref, lam_ref, o_ref):
    x = xr_ref[...]
    seq = x.shape[0]
    row = lax.broadcasted_iota(jnp.int32, x.shape, 0)
    w = cw_ref[...]
    xc = x * w[CONV_W - 1:CONV_W, :] + cb_ref[...]
    for d in range(1, CONV_W):
        xc = xc + _row_shift(x, d, 0.0, row) * w[CONV_W - 1 - d:CONV_W - d, :]
    xcb = xc.astype(BF16)
    gate_r = jax.nn.sigmoid(jnp.dot(xcb, wa_ref[...], preferred_element_type=F32) + ba_ref[...])
    gate_i = jax.nn.sigmoid(jnp.dot(xcb, wx_ref[...], preferred_element_type=F32) + bx_ref[...])
    log_a = -LRU_C * gate_r * _softplus(-lam_ref[...])
    a = jnp.exp(log_a)
    b = jnp.sqrt(-jnp.tanh(log_a) * (a * a + 1.0)) * (gate_i * xc)
    d = 1
    while d < seq:
        b = a * _row_shift(b, d, 0.0, row) + b
        a = a * _row_shift(a, d, 1.0, row)
        d *= 2
    o_ref[...] = (b * jax.nn.gelu(xg_ref[...], approximate=True)).astype(o_ref.dtype)


def _lru(xr, xg, cw, cb, wa_bd, ba, wx_bd, bx, lam, bsz, seq):
    nch = LRU_WIDTH // V7X_LANES

    def xmap(b, c):
        return (b, c)

    def pmap(b, c):
        return (0, c)

    def wmap(b, c):
        return (c, 0, 0)

    vec = pl.BlockSpec((1, V7X_LANES), pmap)
    blk = pl.BlockSpec((None, V7X_LANES, V7X_LANES), wmap)
    return pl.pallas_call(
        _lru_kernel, grid=(bsz, nch),
        in_specs=[pl.BlockSpec((seq, V7X_LANES), xmap), pl.BlockSpec((seq, V7X_LANES), xmap),
                  pl.BlockSpec((CONV_W, V7X_LANES), pmap), vec, blk, vec, blk, vec, vec],
        out_specs=pl.BlockSpec((seq, V7X_LANES), xmap),
        out_shape=jax.ShapeDtypeStruct(xr.shape, BF16),
        compiler_params=_params("parallel", "parallel"), name="rglru")(
            xr, xg, cw, cb, wa_bd, ba, wx_bd, bx, lam)


def _block_diag_pairs(w):
    n, c, _ = w.shape
    z = jnp.zeros((n // 2, c, c), w.dtype)
    top = jnp.concatenate([w[0::2], z], axis=2)
    bot = jnp.concatenate([z, w[1::2]], axis=2)
    return jnp.concatenate([top, bot], axis=1).astype(BF16)


def _res_matmul_kernel(*refs, n_in):
    h_ref = refs[0]
    o_ref = refs[-1]
    acc = h_ref[...]
    for i in range(n_in):
        acc = acc + jnp.dot(refs[1 + 2 * i][...], refs[2 + 2 * i][...], preferred_element_type=F32)
    o_ref[...] = acc


def _res_matmul(h, xs, ws, tm):
    m, d = h.shape
    in_specs = [pl.BlockSpec((tm, d), lambda i: (i, 0))]
    args = [h]
    for x, w in zip(xs, ws):
        in_specs.append(pl.BlockSpec((tm, x.shape[1]), lambda i: (i, 0)))
        in_specs.append(pl.BlockSpec(w.shape, lambda i: (0, 0)))
        args += [x, w]
    return pl.pallas_call(
        functools.partial(_res_matmul_kernel, n_in=len(xs)), grid=(m // tm,),
        in_specs=in_specs, out_specs=pl.BlockSpec((tm, d), lambda i: (i, 0)),
        out_shape=jax.ShapeDtypeStruct(h.shape, F32),
        compiler_params=_params("parallel"), name="res_matmul")(*args)


def _ffn_kernel(h_ref, g_ref, wg_ref, wu_ref, wd_ref, o_ref, xn_ref, acc_ref):
    k = pl.program_id(1)

    @pl.when(k == 0)
    def _():
        xn_ref[...] = _rms(h_ref[...], g_ref[...]).astype(BF16)
        acc_ref[...] = jnp.zeros_like(acc_ref)

    xn = xn_ref[...]
    gate = jnp.dot(xn, wg_ref[...], preferred_element_type=F32)
    up = jnp.dot(xn, wu_ref[...], preferred_element_type=F32)
    act = (_silu(gate) * up).astype(BF16)
    acc_ref[...] += jnp.dot(act, wd_ref[...], preferred_element_type=F32)

    @pl.when(k == pl.num_programs(1) - 1)
    def _():
        o_ref[...] = h_ref[...] + acc_ref[...]


def _ffn(h, gain, wg, wu, wd, tm, tf):
    m, d = h.shape
    ff = wg.shape[1]
    return pl.pallas_call(
        _ffn_kernel, grid=(m // tm, ff // tf),
        in_specs=[pl.BlockSpec((tm, d), lambda i, k: (i, 0)),
                  pl.BlockSpec((1, d), lambda i, k: (0, 0)),
                  pl.BlockSpec((d, tf), lambda i, k: (0, k)),
                  pl.BlockSpec((d, tf), lambda i, k: (0, k)),
                  pl.BlockSpec((tf, d), lambda i, k: (k, 0))],
        out_specs=pl.BlockSpec((tm, d), lambda i, k: (i, 0)),
        out_shape=jax.ShapeDtypeStruct(h.shape, F32),
        scratch_shapes=[pltpu.VMEM((tm, d), BF16), pltpu.VMEM((tm, d), F32)],
        compiler_params=_params("parallel", "arbitrary"), name="ffn")(h, gain, wg, wu, wd)


def _ple_kernel(*refs, final):
    if final:
        h_ref, p_ref, gn_ref, gw_ref, proj_ref, fn_ref, o_ref = refs
    else:
        h_ref, p_ref, gn_ref, gw_ref, proj_ref, o_ref = refs
    h = h_ref[...]
    emb = jnp.dot(p_ref[...].astype(BF16), proj_ref[...], preferred_element_type=F32)
    hn = _rms(h, gn_ref[...]).astype(BF16)
    gate = jax.nn.sigmoid(jnp.dot(hn, gw_ref[...], preferred_element_type=F32))
    out = h + gate * emb
    if final:
        out = _rms(out, fn_ref[...])
    o_ref[...] = out


def _ple(h, p, gate_norm, gate_w, proj, tm, final_norm=None):
    m, d = h.shape
    pd = p.shape[1]
    final = final_norm is not None
    in_specs = [pl.BlockSpec((tm, d), lambda i: (i, 0)),
                pl.BlockSpec((tm, pd), lambda i: (i, 0)),
                pl.BlockSpec((1, d), lambda i: (0, 0)),
                pl.BlockSpec((d, d), lambda i: (0, 0)),
                pl.BlockSpec((pd, d), lambda i: (0, 0))]
    args = [h, p, gate_norm, gate_w, proj]
    if final:
        in_specs.append(pl.BlockSpec((1, d), lambda i: (0, 0)))
        args.append(final_norm)
    return pl.pallas_call(
        functools.partial(_ple_kernel, final=final), grid=(m // tm,),
        in_specs=in_specs, out_specs=pl.BlockSpec((tm, d), lambda i: (i, 0)),
        out_shape=jax.ShapeDtypeStruct(h.shape, F32),
        compiler_params=_params("parallel"), name="ple")(*args)


def _expand_heads(v, n_heads):
    rows = v.shape[0]
    lane = lax.broadcasted_iota(jnp.int32, (rows, V7X_LANES), 1)
    per = V7X_LANES // SSD_HEAD_DIM
    chunks = []
    for c in range(n_heads // per):
        a = jnp.broadcast_to(v[:, per * c:per * c + 1], (rows, V7X_LANES))
        b = jnp.broadcast_to(v[:, per * c + 1:per * c + 2], (rows, V7X_LANES))
        chunks.append(jnp.where(lane < SSD_HEAD_DIM, a, b))
    return jnp.concatenate(chunks, axis=1)


def _ssd_kernel(xbc_ref, z_ref, dt_ref, cw_ref, cb_ref, dtb_ref, alog_ref, dskip_ref, gn_ref,
                y_ref, state_ref, halo_ref, *, inner, n_heads):
    ci = pl.program_id(1)
    cl = SSD_CHUNK
    gs = SSD_GROUPS * SSD_STATE
    hg = n_heads // SSD_GROUPS
    gw = inner // SSD_GROUPS

    @pl.when(ci == 0)
    def _():
        state_ref[...] = jnp.zeros_like(state_ref)
        halo_ref[...] = jnp.zeros_like(halo_ref)

    xraw = xbc_ref[...]
    halo = halo_ref[...]
    w = cw_ref[...]
    row8 = lax.broadcasted_iota(jnp.int32, halo.shape, 0)
    conv = xraw * w[CONV_W - 1:CONV_W, :] + cb_ref[...]
    for d in range(1, CONV_W):
        rolled = pltpu.roll(xraw, d, 0)
        top = jnp.where(row8 >= d, rolled[0:V7X_SUBLANES, :], pltpu.roll(halo, d, 0))
        shifted = jnp.concatenate([top, rolled[V7X_SUBLANES:, :]], axis=0)
        conv = conv + shifted * w[CONV_W - 1 - d:CONV_W - d, :]
    halo_ref[...] = xraw[cl - V7X_SUBLANES:cl, :]
    xc = _silu(conv)
    xs = xc[:, 0:inner]
    bm = xc[:, inner:inner + gs]
    cm = xc[:, inner + gs:inner + 2 * gs]

    dt = _softplus(dt_ref[...] + dtb_ref[...])
    da = dt * (-jnp.exp(alog_ref[...]))
    rowl = lax.broadcasted_iota(jnp.int32, da.shape, 0)
    cum = da
    d = 1
    while d < cl:
        cum = cum + _row_shift(cum, d, 0.0, rowl)
        d *= 2
    cum_t = cum.T
    dt_x = _expand_heads(dt, n_heads)
    cum_x = _expand_heads(cum, n_heads)
    last_x = cum_x[cl - 1:cl, :]
    xdt = xs * dt_x
    xdt_b = xdt.astype(BF16)
    tail_xdt = (jnp.exp(last_x - cum_x) * xdt).astype(BF16)
    exp_cum_x = jnp.exp(cum_x)
    state_decay = jnp.exp(last_x)

    tril = (lax.broadcasted_iota(jnp.int32, (cl, cl), 0) >= lax.broadcasted_iota(jnp.int32, (cl, cl), 1))
    lane = lax.broadcasted_iota(jnp.int32, (cl, V7X_LANES), 1)
    per = V7X_LANES // SSD_HEAD_DIM
    nt = (((1,), (1,)), ((), ()))
    dskip = dskip_ref[...]
    gn = gn_ref[...]
    for g in range(SSD_GROUPS):
        cg = cm[:, g * SSD_STATE:(g + 1) * SSD_STATE].astype(BF16)
        bg_f = bm[:, g * SSD_STATE:(g + 1) * SSD_STATE]
        cb = lax.dot_general(cg, bg_f.astype(BF16), nt, preferred_element_type=F32)
        st = state_ref[:, g * gw:(g + 1) * gw]
        y_off = jnp.dot(cg, st.astype(BF16), preferred_element_type=F32) * exp_cum_x[:, g * gw:(g + 1) * gw]
        diag = []
        for c in range(hg // per):
            pair = []
            for half in range(per):
                h = g * hg + c * per + half
                seg = cum[:, h:h + 1] - cum_t[h:h + 1, :]
                decay = jnp.exp(jnp.where(tril, seg, -jnp.inf))
                lo = g * gw + c * V7X_LANES
                pair.append(jnp.dot((cb * decay).astype(BF16), xdt_b[:, lo:lo + V7X_LANES],
                                    preferred_element_type=F32))
            diag.append(jnp.where(lane < SSD_HEAD_DIM, pair[0], pair[1]))
        y_diag = jnp.concatenate(diag, axis=1)
        new_st = st * state_decay[:, g * gw:(g + 1) * gw] + jnp.dot(
            bg_f.T.astype(BF16), tail_xdt[:, g * gw:(g + 1) * gw], preferred_element_type=F32)
        state_ref[:, g * gw:(g + 1) * gw] = new_st

        y = y_diag + y_off + dskip[:, g * gw:(g + 1) * gw] * xs[:, g * gw:(g + 1) * gw]
        y = y * _silu(z_ref[:, g * gw:(g + 1) * gw])
        y = y * lax.rsqrt(jnp.mean(y * y, axis=-1, keepdims=True) + NORM_EPS)
        y_ref[:, g * gw:(g + 1) * gw] = (y * gn[:, g * gw:(g + 1) * gw]).astype(y_ref.dtype)


def _ssd(xbc, z, dt, cw, cb, dtb, alog, dskip_x, gnorm, bsz, seq, inner, n_heads):
    nc = seq // SSD_CHUNK
    cl = SSD_CHUNK
    cch = xbc.shape[1]

    def xmap(b, c):
        return (b * nc + c, 0)

    def pmap(b, c):
        return (0, 0)

    return pl.pallas_call(
        functools.partial(_ssd_kernel, inner=inner, n_heads=n_heads), grid=(bsz, nc),
        in_specs=[pl.BlockSpec((cl, cch), xmap), pl.BlockSpec((cl, inner), xmap),
                  pl.BlockSpec((cl, V7X_LANES), xmap),
                  pl.BlockSpec((CONV_W, cch), pmap), pl.BlockSpec((1, cch), pmap),
                  pl.BlockSpec((1, V7X_LANES), pmap), pl.BlockSpec((1, V7X_LANES), pmap),
                  pl.BlockSpec((1, inner), pmap), pl.BlockSpec((1, inner), pmap)],
        out_specs=pl.BlockSpec((cl, inner), xmap),
        out_shape=jax.ShapeDtypeStruct((bsz * seq, inner), BF16),
        scratch_shapes=[pltpu.VMEM((SSD_STATE, inner), F32), pltpu.VMEM((V7X_SUBLANES, cch), F32)],
        compiler_params=_params("parallel", "arbitrary"), name="ssd")(
            xbc, z, dt, cw, cb, dtb, alog, dskip_x, gnorm)


def _router_kernel(h_ref, g_ref, r_ref, gates_ref):
    xn = _rms(h_ref[...], g_ref[...])
    logits = jnp.dot(xn, r_ref[...], preferred_element_type=F32, precision=lax.Precision.HIGHEST)
    lane = lax.broadcasted_iota(jnp.int32, logits.shape, 1)
    logits = jnp.where(lane < N_EXPERTS, logits, -jnp.inf)
    m1 = jnp.max(logits, axis=-1, keepdims=True)
    i1 = jnp.min(jnp.where(logits == m1, lane, V7X_LANES), axis=-1, keepdims=True)
    rest = jnp.where(lane == i1, -jnp.inf, logits)
    m2 = jnp.max(rest, axis=-1, keepdims=True)
    i2 = jnp.min(jnp.where(rest == m2, lane, V7X_LANES), axis=-1, keepdims=True)
    e2 = jnp.exp(m2 - m1)
    w1 = 1.0 / (1.0 + e2)
    w2 = e2 / (1.0 + e2)
    gates_ref[...] = jnp.where(lane == i1, w1, jnp.where(lane == i2, w2, 0.0))


def _router(h, gain, router_p, tm):
    m, d = h.shape
    return pl.pallas_call(
        _router_kernel, grid=(m // tm,),
        in_specs=[pl.BlockSpec((tm, d), lambda i: (i, 0)),
                  pl.BlockSpec((1, d), lambda i: (0, 0)),
                  pl.BlockSpec((d, V7X_LANES), lambda i: (0, 0))],
        out_specs=pl.BlockSpec((tm, V7X_LANES), lambda i: (i, 0)),
        out_shape=jax.ShapeDtypeStruct((m, V7X_LANES), F32),
        compiler_params=_params("parallel"), name="router")(h, gain, router_p)


def _moe_kernel(h_ref, g_ref, gates_ref, wg_ref, wu_ref, wd_ref, o_ref, xn_ref, acc_ref):
    e = pl.program_id(1)
    k = pl.program_id(2)

    @pl.when(jnp.logical_and(e == 0, k == 0))
    def _():
        xn_ref[...] = _rms(h_ref[...], g_ref[...]).astype(BF16)
        acc_ref[...] = jnp.zeros_like(acc_ref)

    xn = xn_ref[...]
    gate = jnp.dot(xn, wg_ref[...], preferred_element_type=F32)
    up = jnp.dot(xn, wu_ref[...], preferred_element_type=F32)
    act = (_silu(gate) * up).astype(BF16)
    gates = gates_ref[...]
    lane = lax.broadcasted_iota(jnp.int32, gates.shape, 1)
    ge = jnp.sum(jnp.where(lane == e, gates, 0.0), axis=-1, keepdims=True)
    acc_ref[...] += ge * jnp.dot(act, wd_ref[...], preferred_element_type=F32)

    @pl.when(jnp.logical_and(e == pl.num_programs(1) - 1, k == pl.num_programs(2) - 1))
    def _():
        o_ref[...] = h_ref[...] + acc_ref[...]


def _moe(h, gain, gates, wg, wu, wd, tm, tf):
    m, d = h.shape
    ne, _, ff = wg.shape
    return pl.pallas_call(
        _moe_kernel, grid=(m // tm, ne, ff // tf),
        in_specs=[pl.BlockSpec((tm, d), lambda i, e, k: (i, 0)),
                  pl.BlockSpec((1, d), lambda i, e, k: (0, 0)),
                  pl.BlockSpec((tm, V7X_LANES), lambda i, e, k: (i, 0)),
                  pl.BlockSpec((None, d, tf), lambda i, e, k: (e, 0, k)),
                  pl.BlockSpec((None, d, tf), lambda i, e, k: (e, 0, k)),
                  pl.BlockSpec((None, tf, d), lambda i, e, k: (e, k, 0))],
        out_specs=pl.BlockSpec((tm, d), lambda i, e, k: (i, 0)),
        out_shape=jax.ShapeDtypeStruct(h.shape, F32),
        scratch_shapes=[pltpu.VMEM((tm, d), BF16), pltpu.VMEM((tm, d), F32)],
        compiler_params=_params("parallel", "arbitrary", "arbitrary"), name="moe")(
            h, gain, gates, wg, wu, wd)


def _row(v):
    return v.reshape(1, -1).astype(F32)


def _pad_cols(w, width):
    return jnp.pad(w, ((0, 0), (0, width - w.shape[1])))


def _tile(m, want):
    t = min(want, m)
    while m % t:
        t //= 2
    return t


def _even_layer(h, p_i, bsz, seq, mix_norm, w_in, conv_w, conv_b, wa, ba, wx, bx, lam,
                w_out, ffn_norm, ffn_wg, ffn_wu, ffn_wd, ple_proj, ple_gate_norm, ple_gate):
    m = h.shape[0]
    qw = A_HEADS * HEAD_DIM
    kw = A_KV_HEADS * HEAD_DIM
    iw = IDX_HEADS * IDX_DIM
    o = 0
    wq = w_in[:, o:o + qw]; o += qw
    wk = w_in[:, o:o + kw]; o += kw
    wv = w_in[:, o:o + kw]; o += kw
    wqi = w_in[:, o:o + iw]; o += iw
    wki = w_in[:, o:o + IDX_DIM]; o += IDX_DIM
    wwi = w_in[:, o:o + IDX_HEADS]; o += IDX_HEADS
    wxg = w_in[:, o:o + LRU_WIDTH]; o += LRU_WIDTH
    wxr = w_in[:, o:o + LRU_WIDTH]

    def dup_heads(wm):
        parts = []
        for g in range(wm.shape[1] // HEAD_DIM):
            blk = wm[:, g * HEAD_DIM:(g + 1) * HEAD_DIM]
            parts += [blk, blk]
        return jnp.concatenate(parts, axis=1)

    w_all = jnp.concatenate([wq, dup_heads(wk), dup_heads(wv), wqi, dup_heads(wki),
                             _pad_cols(wwi, V7X_LANES), wxg, wxr], axis=1).astype(BF16)
    sections = ((qw, True), (2 * kw, True), (2 * kw, False), (iw, True), (2 * IDX_DIM, True),
                (V7X_LANES, False), (LRU_WIDTH, False), (LRU_WIDTH, False))
    dtypes = (BF16, BF16, BF16, BF16, BF16, F32, F32, F32)
    tm = _tile(seq, 512)
    q, kk, vv, qi, kiki, wi, xg, xr = _in_proj(h, _row(mix_norm), w_all, sections, dtypes, tm,
                                               rope_tables=_rope_tables(seq), seq=seq)
    o_a = _dsa(q, qi, wi, kk, vv, kiki, bsz, seq)
    o_b = _lru(xr, xg, conv_w, _row(conv_b), _block_diag_pairs(wa), _row(ba),
               _block_diag_pairs(wx), _row(bx), _row(lam), bsz, seq)
    w_out_b = w_out.astype(BF16)
    h = _res_matmul(h, [o_a, o_b], [w_out_b[:qw], w_out_b[qw:]], _tile(m, 512))
    h = _ffn(h, _row(ffn_norm), ffn_wg.astype(BF16), ffn_wu.astype(BF16), ffn_wd.astype(BF16),
             _tile(m, 1024), 512)
    return _ple(h, p_i, _row(ple_gate_norm), ple_gate.astype(BF16), ple_proj.astype(BF16), _tile(m, 512))


def _odd_layer(h, p_i, bsz, seq, mix_norm, w_in, conv_w, conv_b, dt_bias, a_log, d_skip, gnorm, w_out,
               moe_norm, router, exp_wg, exp_wu, exp_wd, ple_proj, ple_gate_norm, ple_gate, final_norm):
    m = h.shape[0]
    inner = gnorm.shape[0]
    n_heads = dt_bias.shape[0]
    cch = conv_w.shape[1]
    w_z = w_in[:, :inner]
    w_xbc = w_in[:, inner:inner + cch]
    w_dt = _pad_cols(w_in[:, inner + cch:], V7X_LANES)
    w_all = jnp.concatenate([w_z, w_xbc, w_dt], axis=1).astype(BF16)
    sections = ((inner, False), (cch, False), (V7X_LANES, False))
    z, xbc, dt = _in_proj(h, _row(mix_norm), w_all, sections, (F32, F32, F32), _tile(m, 256))
    pad = V7X_LANES - n_heads
    y = _ssd(xbc, z, dt, conv_w, _row(conv_b), _row(jnp.pad(dt_bias, (0, pad))),
             _row(jnp.pad(a_log, (0, pad))), _row(jnp.repeat(d_skip, SSD_HEAD_DIM)), _row(gnorm),
             bsz, seq, inner, n_heads)
    h = _res_matmul(h, [y], [w_out.astype(BF16)], _tile(m, 512))
    gates = _router(h, _row(moe_norm), _pad_cols(router, V7X_LANES), _tile(m, 512))
    h = _moe(h, _row(moe_norm), gates, exp_wg.astype(BF16), exp_wu.astype(BF16), exp_wd.astype(BF16),
             _tile(m, 1024), 512)
    return _ple(h, p_i, _row(ple_gate_norm), ple_gate.astype(BF16), ple_proj.astype(BF16), _tile(m, 512),
                final_norm=_row(final_norm))


def kernel(x, p, e_mix_norm, e_w_in, e_lru_conv_w, e_lru_conv_b, e_lru_wa, e_lru_ba, e_lru_wx, e_lru_bx, e_lru_lambda, e_w_out, e_ffn_norm, e_ffn_wg, e_ffn_wu, e_ffn_wd, e_ple_proj, e_ple_gate_norm, e_ple_gate, o_mix_norm, o_w_in, o_conv_w, o_conv_b, o_dt_bias, o_a_log, o_d_skip, o_gnorm, o_w_out, o_moe_norm, o_router, o_exp_wg, o_exp_wu, o_exp_wd, o_ple_proj, o_ple_gate_norm, o_ple_gate, final_norm):
    bsz, seq, d = x.shape
    m = bsz * seq
    h = x.reshape(m, d)
    pf = p.reshape(p.shape[0], m, p.shape[-1])
    h = _even_layer(h, pf[0], bsz, seq, e_mix_norm, e_w_in, e_lru_conv_w, e_lru_conv_b, e_lru_wa, e_lru_ba,
                    e_lru_wx, e_lru_bx, e_lru_lambda, e_w_out, e_ffn_norm, e_ffn_wg, e_ffn_wu, e_ffn_wd,
                    e_ple_proj, e_ple_gate_norm, e_ple_gate)
    h = _odd_layer(h, pf[1], bsz, seq, o_mix_norm, o_w_in, o_conv_w, o_conv_b, o_dt_bias, o_a_log, o_d_skip,
                   o_gnorm, o_w_out, o_moe_norm, o_router, o_exp_wg, o_exp_wu, o_exp_wd,
                   o_ple_proj, o_ple_gate_norm, o_ple_gate, final_norm)
    return h.reshape(bsz, seq, d)
```

```python
import functools
import math

import jax
import jax.numpy as jnp
from jax import lax
from jax.experimental import pallas as pl
from jax.experimental.pallas import tpu as pltpu

F32 = jnp.float32
BF16 = jnp.bfloat16

V7X_LANES = 128
V7X_SUBLANES = 8
V7X_VMEM_LIMIT_BYTES = 56 * 1024 * 1024

NORM_EPS = 1e-6
HEAD_DIM = 64
A_HEADS = 8
A_KV_HEADS = 2
ROPE_DIM = HEAD_DIM // 4
ROPE_THETA = 500000.0
IDX_HEADS = 4
IDX_DIM = 64
TOPK_MAX = 256
Q_BLOCK = 128
LRU_WIDTH = 512
LRU_BLOCKS = 8
CONV_W = 4
LRU_C = 8.0
SSD_HEAD_DIM = 64
SSD_GROUPS = 4
SSD_STATE = 128
SSD_CHUNK = 128
N_EXPERTS = 8
F32_MAX = float(jnp.finfo(jnp.float32).max)


def _params(*semantics):
    return pltpu.CompilerParams(dimension_semantics=semantics,
                                vmem_limit_bytes=V7X_VMEM_LIMIT_BYTES)


def _rms(x, g):
    ms = jnp.mean(x * x, axis=-1, keepdims=True)
    return x * lax.rsqrt(ms + NORM_EPS) * g


def _silu(x):
    return x * jax.nn.sigmoid(x)


def _softplus(x):
    return jnp.maximum(x, 0.0) + jnp.log1p(jnp.exp(-jnp.abs(x)))


def _row_shift(v, d, fill, row):
    return jnp.where(row >= d, pltpu.roll(v, d, 0), fill)


def _in_proj_kernel(*refs, sections, use_rope):
    if use_rope:
        x_ref, g_ref, w_ref, cos_ref, sa_ref, sb_ref = refs[:6]
        out_refs = refs[6:]
    else:
        x_ref, g_ref, w_ref = refs[:3]
        out_refs = refs[3:]
    xn = _rms(x_ref[...], g_ref[...]).astype(BF16)
    off = 0
    for out_ref, (width, rope) in zip(out_refs, sections):
        y = jnp.dot(xn, w_ref[:, off:off + width], preferred_element_type=F32)
        if rope:
            c, sa, sb = cos_ref[...], sa_ref[...], sb_ref[...]
            for ch in range(width // V7X_LANES):
                lo = ch * V7X_LANES
                yc = y[:, lo:lo + V7X_LANES]
                yr = (yc * c + pltpu.roll(yc, V7X_LANES - ROPE_DIM // 2, 1) * sa
                      + pltpu.roll(yc, ROPE_DIM // 2, 1) * sb)
                out_ref[:, lo:lo + V7X_LANES] = yr.astype(out_ref.dtype)
        else:
            out_ref[...] = y.astype(out_ref.dtype)
        off += width


def _in_proj(x, gain, w, sections, dtypes, tm, rope_tables=None, seq=None):
    m, d = x.shape
    n = w.shape[1]
    use_rope = rope_tables is not None
    in_specs = [pl.BlockSpec((tm, d), lambda i: (i, 0)),
                pl.BlockSpec((1, d), lambda i: (0, 0)),
                pl.BlockSpec((d, n), lambda i: (0, 0))]
    args = [x, gain, w]
    if use_rope:
        nper = seq // tm
        for t in rope_tables:
            in_specs.append(pl.BlockSpec((tm, V7X_LANES), lambda i: (i % nper, 0)))
            args.append(t)
    out_shape = [jax.ShapeDtypeStruct((m, wd), dt) for (wd, _), dt in zip(sections, dtypes)]
    out_specs = [pl.BlockSpec((tm, wd), lambda i: (i, 0)) for (wd, _) in sections]
    return pl.pallas_call(
        functools.partial(_in_proj_kernel, sections=sections, use_rope=use_rope),
        grid=(m // tm,), in_specs=in_specs, out_specs=out_specs, out_shape=out_shape,
        compiler_params=_params("parallel"), name="in_proj")(*args)


def _rope_tables(seq):
    half = ROPE_DIM // 2
    inv = ROPE_THETA ** (-jnp.arange(half, dtype=F32) / half)
    ang = jnp.arange(seq, dtype=F32)[:, None] * inv[None, :]
    cos, sin = jnp.cos(ang), jnp.sin(ang)
    ones = jnp.ones((seq, HEAD_DIM - ROPE_DIM), F32)
    zeros = jnp.zeros((seq, HEAD_DIM - ROPE_DIM), F32)
    zh = jnp.zeros((seq, half), F32)
    c = jnp.concatenate([cos, cos, ones], axis=1)
    sa = jnp.concatenate([-sin, zh, zeros], axis=1)
    sb = jnp.concatenate([zh, sin, zeros], axis=1)
    rep = V7X_LANES // HEAD_DIM
    return tuple(jnp.tile(t, (1, rep)) for t in (c, sa, sb))


def _dsa_body(q_ref, qi_ref, wi_ref, kk_ref, vv_ref, kiki_ref, o_ref, score_ref, bias_ref,
              *, sk, n_sel):
    t = Q_BLOCK
    j = pl.program_id(1)
    lane = lax.broadcasted_iota(jnp.int32, (t, V7X_LANES), 1)
    lo_half = lane < HEAD_DIM
    row = j * t + lax.broadcasted_iota(jnp.int32, (t, 1), 0)
    col = lax.broadcasted_iota(jnp.int32, (t, sk), 1)
    nt = (((1,), (1,)), ((), ()))

    kiki = kiki_ref[0:sk, :]
    wi = wi_ref[...] * (IDX_HEADS ** -0.5 * IDX_DIM ** -0.5)
    score = jnp.zeros((t, sk), F32)
    for h in range(IDX_HEADS):
        c = h // 2
        qc = qi_ref[:, c * V7X_LANES:(c + 1) * V7X_LANES]
        qm = jnp.where(lo_half if h % 2 == 0 else jnp.logical_not(lo_half), qc, jnp.zeros_like(qc))
        s = lax.dot_general(qm, kiki, nt, preferred_element_type=F32)
        score = score + wi[:, h:h + 1] * jnp.maximum(s, 0.0)
    score_ref[:, 0:sk] = jnp.where(col <= row, score, -jnp.inf)

    def search(i, u):
        u_try = u | lax.shift_left(jnp.int32(1), 31 - i)
        bits = jnp.where(u_try < 0, u_try ^ jnp.int32(-2 ** 31), ~u_try)
        cand = lax.bitcast_convert_type(bits, F32)
        cnt = jnp.sum(jnp.where(score_ref[:, 0:sk] >= cand, 1.0, 0.0), axis=1, keepdims=True)
        return jnp.where(cnt >= n_sel, u_try, u)

    u = lax.fori_loop(0, 32, search, jnp.zeros((t, 1), jnp.int32))
    bits = jnp.where(u < 0, u ^ jnp.int32(-2 ** 31), ~u)
    thr = lax.bitcast_convert_type(bits, F32)
    thr = jnp.where(row + 1 <= n_sel, -F32_MAX, thr)

    cnt_gt = jnp.sum(jnp.where(score_ref[:, 0:sk] > thr, 1.0, 0.0), axis=1, keepdims=True)
    need = n_sel - cnt_gt
    tri = (lax.broadcasted_iota(jnp.int32, (V7X_LANES, V7X_LANES), 0)
           <= lax.broadcasted_iota(jnp.int32, (V7X_LANES, V7X_LANES), 1))
    tri = jnp.where(tri, 1.0, 0.0).astype(BF16)
    carry = jnp.zeros((t, 1), F32)
    for c in range(sk // V7X_LANES):
        sc = score_ref[:, c * V7X_LANES:(c + 1) * V7X_LANES]
        eq = sc == thr
        pre = jnp.dot(jnp.where(eq, 1.0, 0.0).astype(BF16), tri, preferred_element_type=F32) + carry
        carry = pre[:, V7X_LANES - 1:V7X_LANES]
        tie = jnp.where(eq, jnp.where(pre <= need, 0.0, -jnp.inf), -jnp.inf)
        bias_ref[:, c * V7X_LANES:(c + 1) * V7X_LANES] = jnp.where(sc > thr, 0.0, tie)

    scale = HEAD_DIM ** -0.5
    heads_per_chunk = V7X_LANES // HEAD_DIM
    groups = A_HEADS // A_KV_HEADS
    for c in range(A_HEADS // heads_per_chunk):
        g = (c * heads_per_chunk) // groups
        kk = kk_ref[0:sk, g * V7X_LANES:(g + 1) * V7X_LANES]
        vv = vv_ref[0:sk, g * V7X_LANES:(g + 1) * V7X_LANES]
        qc = q_ref[:, c * V7X_LANES:(c + 1) * V7X_LANES]
        outs = []
        for half in range(heads_per_chunk):
            qm = jnp.where(lo_half if half == 0 else jnp.logical_not(lo_half), qc, jnp.zeros_like(qc))
            lg = lax.dot_general(qm, kk, nt, preferred_element_type=F32) * scale + bias_ref[:, 0:sk]
            mx = jnp.max(lg, axis=-1, keepdims=True)
            p = jnp.exp(lg - mx)
            den = jnp.sum(p, axis=-1, keepdims=True)
            outs.append(jnp.dot(p.astype(BF16), vv, preferred_element_type=F32) / den)
        o_ref[:, c * V7X_LANES:(c + 1) * V7X_LANES] = jnp.where(lo_half, outs[0], outs[1]).astype(o_ref.dtype)


def _dsa_kernel(q_ref, qi_ref, wi_ref, kk_ref, vv_ref, kiki_ref, o_ref, score_ref, bias_ref,
                *, classes, n_sel):
    j = pl.program_id(1)
    prev = 0
    for jmax, sk in classes:
        @pl.when(jnp.logical_and(j >= prev, j < jmax))
        def _(sk=sk):
            _dsa_body(q_ref, qi_ref, wi_ref, kk_ref, vv_ref, kiki_ref, o_ref, score_ref, bias_ref,
                      sk=sk, n_sel=n_sel)
        prev = jmax


def _dsa_classes(nblk):
    n_cls = min(4, nblk)
    bounds = sorted({-(-nblk * (i + 1) // n_cls) for i in range(n_cls)})
    return tuple((b, b * Q_BLOCK) for b in bounds)


def _dsa(q, qi, wi, kk, vv, kiki, bsz, seq):
    n_sel = min(TOPK_MAX, seq // 4)
    nblk = seq // Q_BLOCK
    t = Q_BLOCK

    def qmap(b, j):
        return (b * nblk + j, 0)

    def kmap(b, j):
        return (b, 0)

    return pl.pallas_call(
        functools.partial(_dsa_kernel, classes=_dsa_classes(nblk), n_sel=float(n_sel)),
        grid=(bsz, nblk),
        in_specs=[pl.BlockSpec((t, q.shape[1]), qmap),
                  pl.BlockSpec((t, qi.shape[1]), qmap),
                  pl.BlockSpec((t, wi.shape[1]), qmap),
                  pl.BlockSpec((seq, kk.shape[1]), kmap),
                  pl.BlockSpec((seq, vv.shape[1]), kmap),
                  pl.BlockSpec((seq, kiki.shape[1]), kmap)],
        out_specs=pl.BlockSpec((t, q.shape[1]), qmap),
        out_shape=jax.ShapeDtypeStruct(q.shape, BF16),
        scratch_shapes=[pltpu.VMEM((t, seq), F32), pltpu.VMEM((t, seq), F32)],
        compiler_params=_params("parallel", "arbitrary"), name="dsa")(q, qi, wi, kk, vv, kiki)


def _lru_kernel(xr_ref, xg_ref, cw_ref, cb_ref, wa_ref, ba_ref, wx_ref, bx_ref, lam_ref, o_ref):
    x = xr_ref[...]
    seq = x.shape[0]
    row = lax.broadcasted_iota(jnp.int32, x.shape, 0)
    w = cw_ref[...]
    xc = x * w[CONV_W - 1:CONV_W, :] + cb_ref[...]
    for d in range(1, CONV_W):
        xc = xc + _row_shift(x, d, 0.0, row) * w[CONV_W - 1 - d:CONV_W - d, :]
    xcb = xc.astype(BF16)
    gate_r = jax.nn.sigmoid(jnp.dot(xcb, wa_ref[...], preferred_element_type=F32) + ba_ref[...])
    gate_i = jax.nn.sigmoid(jnp.dot(xcb, wx_ref[...], preferred_element_type=F32) + bx_ref[...])
    log_a = -LRU_C * gate_r * _softplus(-lam_ref[...])
    a = jnp.exp(log_a)
    b = jnp.sqrt(-jnp.tanh(log_a) * (a * a + 1.0)) * (gate_i * xc)
    d = 1
    while d < seq:
        b = a * _row_shift(b, d, 0.0, row) + b
        a = a * _row_shift(a, d, 1.0, row)
        d *= 2
    o_ref[...] = (b * jax.nn.gelu(xg_ref[...], approximate=True)).astype(o_ref.dtype)


def _lru(xr, xg, cw, cb, wa_bd, ba, wx_bd, bx, lam, bsz, seq):
    nch = LRU_WIDTH // V7X_LANES

    def xmap(b, c):
        return (b, c)

    def pmap(b, c):
        return (0, c)

    def wmap(b, c):
        return (c, 0, 0)

    vec = pl.BlockSpec((1, V7X_LANES), pmap)
    blk = pl.BlockSpec((None, V7X_LANES, V7X_LANES), wmap)
    return pl.pallas_call(
        _lru_kernel, grid=(bsz, nch),
        in_specs=[pl.BlockSpec((seq, V7X_LANES), xmap), pl.BlockSpec((seq, V7X_LANES), xmap),
                  pl.BlockSpec((CONV_W, V7X_LANES), pmap), vec, blk, vec, blk, vec, vec],
        out_specs=pl.BlockSpec((seq, V7X_LANES), xmap),
        out_shape=jax.ShapeDtypeStruct(xr.shape, BF16),
        compiler_params=_params("parallel", "parallel"), name="rglru")(
            xr, xg, cw, cb, wa_bd, ba, wx_bd, bx, lam)


def _block_diag_pairs(w):
    n, c, _ = w.shape
    z = jnp.zeros((n // 2, c, c), w.dtype)
    top = jnp.concatenate([w[0::2], z], axis=2)
    bot = jnp.concatenate([z, w[1::2]], axis=2)
    return jnp.concatenate([top, bot], axis=1).astype(BF16)


def _res_matmul_kernel(*refs, n_in):
    h_ref = refs[0]
    o_ref = refs[-1]
    acc = h_ref[...]
    for i in range(n_in):
        acc = acc + jnp.dot(refs[1 + 2 * i][...], refs[2 + 2 * i][...], preferred_element_type=F32)
    o_ref[...] = acc


def _res_matmul(h, xs, ws, tm):
    m, d = h.shape
    in_specs = [pl.BlockSpec((tm, d), lambda i: (i, 0))]
    args = [h]
    for x, w in zip(xs, ws):
        in_specs.append(pl.BlockSpec((tm, x.shape[1]), lambda i: (i, 0)))
        in_specs.append(pl.BlockSpec(w.shape, lambda i: (0, 0)))
        args += [x, w]
    return pl.pallas_call(
        functools.partial(_res_matmul_kernel, n_in=len(xs)), grid=(m // tm,),
        in_specs=in_specs, out_specs=pl.BlockSpec((tm, d), lambda i: (i, 0)),
        out_shape=jax.ShapeDtypeStruct(h.shape, F32),
        compiler_params=_params("parallel"), name="res_matmul")(*args)


def _ffn_kernel(h_ref, g_ref, wg_ref, wu_ref, wd_ref, o_ref, xn_ref, acc_ref):
    k = pl.program_id(1)

    @pl.when(k == 0)
    def _():
        xn_ref[...] = _rms(h_ref[...], g_ref[...]).astype(BF16)
        acc_ref[...] = jnp.zeros_like(acc_ref)

    xn = xn_ref[...]
    gate = jnp.dot(xn, wg_ref[...], preferred_element_type=F32)
    up = jnp.dot(xn, wu_ref[...], preferred_element_type=F32)
    act = (_silu(gate) * up).astype(BF16)
    acc_ref[...] += jnp.dot(act, wd_ref[...], preferred_element_type=F32)

    @pl.when(k == pl.num_programs(1) - 1)
    def _():
        o_ref[...] = h_ref[...] + acc_ref[...]


def _ffn(h, gain, wg, wu, wd, tm, tf):
    m, d = h.shape
    ff = wg.shape[1]
    return pl.pallas_call(
        _ffn_kernel, grid=(m // tm, ff // tf),
        in_specs=[pl.BlockSpec((tm, d), lambda i, k: (i, 0)),
                  pl.BlockSpec((1, d), lambda i, k: (0, 0)),
                  pl.BlockSpec((d, tf), lambda i, k: (0, k)),
                  pl.BlockSpec((d, tf), lambda i, k: (0, k)),
                  pl.BlockSpec((tf, d), lambda i, k: (k, 0))],
        out_specs=pl.BlockSpec((tm, d), lambda i, k: (i, 0)),
        out_shape=jax.ShapeDtypeStruct(h.shape, F32),
        scratch_shapes=[pltpu.VMEM((tm, d), BF16), pltpu.VMEM((tm, d), F32)],
        compiler_params=_params("parallel", "arbitrary"), name="ffn")(h, gain, wg, wu, wd)


def _ple_kernel(*refs, final):
    if final:
        h_ref, p_ref, gn_ref, gw_ref, proj_ref, fn_ref, o_ref = refs
    else:
        h_ref, p_ref, gn_ref, gw_ref, proj_ref, o_ref = refs
    h = h_ref[...]
    emb = jnp.dot(p_ref[...].astype(BF16), proj_ref[...], preferred_element_type=F32)
    hn = _rms(h, gn_ref[...]).astype(BF16)
    gate = jax.nn.sigmoid(jnp.dot(hn, gw_ref[...], preferred_element_type=F32))
    out = h + gate * emb
    if final:
        out = _rms(out, fn_ref[...])
    o_ref[...] = out


def _ple(h, p, gate_norm, gate_w, proj, tm, final_norm=None):
    m, d = h.shape
    pd = p.shape[1]
    final = final_norm is not None
    in_specs = [pl.BlockSpec((tm, d), lambda i: (i, 0)),
                pl.BlockSpec((tm, pd), lambda i: (i, 0)),
                pl.BlockSpec((1, d), lambda i: (0, 0)),
                pl.BlockSpec((d, d), lambda i: (0, 0)),
                pl.BlockSpec((pd, d), lambda i: (0, 0))]
    args = [h, p, gate_norm, gate_w, proj]
    if final:
        in_specs.append(pl.BlockSpec((1, d), lambda i: (0, 0)))
        args.append(final_norm)
    return pl.pallas_call(
        functools.partial(_ple_kernel, final=final), grid=(m // tm,),
        in_specs=in_specs, out_specs=pl.BlockSpec((tm, d), lambda i: (i, 0)),
        out_shape=jax.ShapeDtypeStruct(h.shape, F32),
        compiler_params=_params("parallel"), name="ple")(*args)


def _expand_heads(v, n_heads):
    rows = v.shape[0]
    lane = lax.broadcasted_iota(jnp.int32, (rows, V7X_LANES), 1)
    per = V7X_LANES // SSD_HEAD_DIM
    chunks = []
    for c in range(n_heads // per):
        a = jnp.broadcast_to(v[:, per * c:per * c + 1], (rows, V7X_LANES))
        b = jnp.broadcast_to(v[:, per * c + 1:per * c + 2], (rows, V7X_LANES))
        chunks.append(jnp.where(lane < SSD_HEAD_DIM, a, b))
    return jnp.concatenate(chunks, axis=1)


def _ssd_kernel(xbc_ref, z_ref, dt_ref, cw_ref, cb_ref, dtb_ref, alog_ref, dskip_ref, gn_ref,
                y_ref, state_ref, halo_ref, *, inner, n_heads):
    ci = pl.program_id(1)
    cl = SSD_CHUNK
    gs = SSD_GROUPS * SSD_STATE
    hg = n_heads // SSD_GROUPS
    gw = inner // SSD_GROUPS

    @pl.when(ci == 0)
    def _():
        state_ref[...] = jnp.zeros_like(state_ref)
        halo_ref[...] = jnp.zeros_like(halo_ref)

    xraw = xbc_ref[...]
    halo = halo_ref[...]
    w = cw_ref[...]
    row8 = lax.broadcasted_iota(jnp.int32, halo.shape, 0)
    conv = xraw * w[CONV_W - 1:CONV_W, :] + cb_ref[...]
    for d in range(1, CONV_W):
        rolled = pltpu.roll(xraw, d, 0)
        top = jnp.where(row8 >= d, rolled[0:V7X_SUBLANES, :], pltpu.roll(halo, d, 0))
        shifted = jnp.concatenate([top, rolled[V7X_SUBLANES:, :]], axis=0)
        conv = conv + shifted * w[CONV_W - 1 - d:CONV_W - d, :]
    halo_ref[...] = xraw[cl - V7X_SUBLANES:cl, :]
    xc = _silu(conv)
    xs = xc[:, 0:inner]
    bm = xc[:, inner:inner + gs]
    cm = xc[:, inner + gs:inner + 2 * gs]

    dt = _softplus(dt_ref[...] + dtb_ref[...])
    da = dt * (-jnp.exp(alog_ref[...]))
    rowl = lax.broadcasted_iota(jnp.int32, da.shape, 0)
    cum = da
    d = 1
    while d < cl:
        cum = cum + _row_shift(cum, d, 0.0, rowl)
        d *= 2
    cum_t = cum.T
    dt_x = _expand_heads(dt, n_heads)
    cum_x = _expand_heads(cum, n_heads)
    last_x = cum_x[cl - 1:cl, :]
    xdt = xs * dt_x
    xdt_b = xdt.astype(BF16)
    tail_xdt = (jnp.exp(last_x - cum_x) * xdt).astype(BF16)
    exp_cum_x = jnp.exp(cum_x)
    state_decay = jnp.exp(last_x)

    tril = (lax.broadcasted_iota(jnp.int32, (cl, cl), 0) >= lax.broadcasted_iota(jnp.int32, (cl, cl), 1))
    lane = lax.broadcasted_iota(jnp.int32, (cl, V7X_LANES), 1)
    per = V7X_LANES // SSD_HEAD_DIM
    nt = (((1,), (1,)), ((), ()))
    dskip = dskip_ref[...]
    gn = gn_ref[...]
    for g in range(SSD_GROUPS):
        cg = cm[:, g * SSD_STATE:(g + 1) * SSD_STATE].astype(BF16)
        bg_f = bm[:, g * SSD_STATE:(g + 1) * SSD_STATE]
        cb = lax.dot_general(cg, bg_f.astype(BF16), nt, preferred_element_type=F32)
        st = state_ref[:, g * gw:(g + 1) * gw]
        y_off = jnp.dot(cg, st.astype(BF16), preferred_element_type=F32) * exp_cum_x[:, g * gw:(g + 1) * gw]
        diag = []
        for c in range(hg // per):
            pair = []
            for half in range(per):
                h = g * hg + c * per + half
                seg = cum[:, h:h + 1] - cum_t[h:h + 1, :]
                decay = jnp.exp(jnp.where(tril, seg, -jnp.inf))
                lo = g * gw + c * V7X_LANES
                pair.append(jnp.dot((cb * decay).astype(BF16), xdt_b[:, lo:lo + V7X_LANES],
                                    preferred_element_type=F32))
            diag.append(jnp.where(lane < SSD_HEAD_DIM, pair[0], pair[1]))
        y_diag = jnp.concatenate(diag, axis=1)
        new_st = st * state_decay[:, g * gw:(g + 1) * gw] + jnp.dot(
            bg_f.T.astype(BF16), tail_xdt[:, g * gw:(g + 1) * gw], preferred_element_type=F32)
        state_ref[:, g * gw:(g + 1) * gw] = new_st

        y = y_diag + y_off + dskip[:, g * gw:(g + 1) * gw] * xs[:, g * gw:(g + 1) * gw]
        y = y * _silu(z_ref[:, g * gw:(g + 1) * gw])
        y = y * lax.rsqrt(jnp.mean(y * y, axis=-1, keepdims=True) + NORM_EPS)
        y_ref[:, g * gw:(g + 1) * gw] = (y * gn[:, g * gw:(g + 1) * gw]).astype(y_ref.dtype)


def _ssd(xbc, z, dt, cw, cb, dtb, alog, dskip_x, gnorm, bsz, seq, inner, n_heads):
    nc = seq // SSD_CHUNK
    cl = SSD_CHUNK
    cch = xbc.shape[1]

    def xmap(b, c):
        return (b * nc + c, 0)

    def pmap(b, c):
        return (0, 0)

    return pl.pallas_call(
        functools.partial(_ssd_kernel, inner=inner, n_heads=n_heads), grid=(bsz, nc),
        in_specs=[pl.BlockSpec((cl, cch), xmap), pl.BlockSpec((cl, inner), xmap),
                  pl.BlockSpec((cl, V7X_LANES), xmap),
                  pl.BlockSpec((CONV_W, cch), pmap), pl.BlockSpec((1, cch), pmap),
                  pl.BlockSpec((1, V7X_LANES), pmap), pl.BlockSpec((1, V7X_LANES), pmap),
                  pl.BlockSpec((1, inner), pmap), pl.BlockSpec((1, inner), pmap)],
        out_specs=pl.BlockSpec((cl, inner), xmap),
        out_shape=jax.ShapeDtypeStruct((bsz * seq, inner), BF16),
        scratch_shapes=[pltpu.VMEM((SSD_STATE, inner), F32), pltpu.VMEM((V7X_SUBLANES, cch), F32)],
        compiler_params=_params("parallel", "arbitrary"), name="ssd")(
            xbc, z, dt, cw, cb, dtb, alog, dskip_x, gnorm)


def _router_kernel(h_ref, g_ref, r_ref, gates_ref):
    xn = _rms(h_ref[...], g_ref[...])
    logits = jnp.dot(xn, r_ref[...], preferred_element_type=F32, precision=lax.Precision.HIGHEST)
    lane = lax.broadcasted_iota(jnp.int32, logits.shape, 1)
    logits = jnp.where(lane < N_EXPERTS, logits, -jnp.inf)
    m1 = jnp.max(logits, axis=-1, keepdims=True)
    i1 = jnp.min(jnp.where(logits == m1, lane, V7X_LANES), axis=-1, keepdims=True)
    rest = jnp.where(lane == i1, -jnp.inf, logits)
    m2 = jnp.max(rest, axis=-1, keepdims=True)
    i2 = jnp.min(jnp.where(rest == m2, lane, V7X_LANES), axis=-1, keepdims=True)
    e2 = jnp.exp(m2 - m1)
    w1 = 1.0 / (1.0 + e2)
    w2 = e2 / (1.0 + e2)
    gates_ref[...] = jnp.where(lane == 0, i1.astype(F32), jnp.where(
        lane == 1, i2.astype(F32), jnp.where(lane == 2, w1, jnp.where(lane == 3, w2, 0.0))))


def _router(h, gain, router_p, tm):
    m, d = h.shape
    return pl.pallas_call(
        _router_kernel, grid=(m // tm,),
        in_specs=[pl.BlockSpec((tm, d), lambda i: (i, 0)),
                  pl.BlockSpec((1, d), lambda i: (0, 0)),
                  pl.BlockSpec((d, V7X_LANES), lambda i: (0, 0))],
        out_specs=pl.BlockSpec((tm, V7X_LANES), lambda i: (i, 0)),
        out_shape=jax.ShapeDtypeStruct((m, V7X_LANES), F32),
        compiler_params=_params("parallel"), name="router")(h, gain, router_p)


def _gather_rows(idx_hbm, src_hbm, idx_smem, dst, sem_idx, sem_rows, tile, n_rows):
    cp = pltpu.make_async_copy(idx_hbm.at[tile], idx_smem, sem_idx)
    cp.start()
    cp.wait()

    def row_copy(r, src_row):
        return pltpu.make_async_copy(src_hbm.at[pl.ds(src_row, 1), :], dst.at[pl.ds(r, 1), :], sem_rows)

    def issue(r, carry):
        row_copy(r, idx_smem[0, r]).start()
        return carry

    def drain(r, carry):
        row_copy(r, 0).wait()
        return carry

    lax.fori_loop(0, n_rows, issue, 0, unroll=8)
    lax.fori_loop(0, n_rows, drain, 0, unroll=8)


def _moe_group_kernel(texp_ref, nused_ref, tok_hbm, h_hbm, g_ref, gate_ref, wg_ref, wu_ref, wd_ref, o_ref,
                      idx_smem, xbuf, xn_ref, acc_ref, sem_idx, sem_rows):
    t = pl.program_id(0)
    k = pl.program_id(1)
    active = t < nused_ref[0]
    last = k == pl.num_programs(1) - 1

    @pl.when(jnp.logical_and(active, k == 0))
    def _():
        _gather_rows(tok_hbm, h_hbm, idx_smem, xbuf, sem_idx, sem_rows, t, xbuf.shape[0])
        xn_ref[...] = _rms(xbuf[...], g_ref[...]).astype(BF16)
        acc_ref[...] = jnp.zeros_like(acc_ref)

    @pl.when(active)
    def _():
        xn = xn_ref[...]
        gate = jnp.dot(xn, wg_ref[...], preferred_element_type=F32)
        up = jnp.dot(xn, wu_ref[...], preferred_element_type=F32)
        act = (_silu(gate) * up).astype(BF16)
        acc_ref[...] += jnp.dot(act, wd_ref[...], preferred_element_type=F32)

    @pl.when(jnp.logical_and(active, last))
    def _():
        o_ref[...] = acc_ref[...] * gate_ref[...]

    @pl.when(jnp.logical_and(jnp.logical_not(active), last))
    def _():
        o_ref[...] = jnp.zeros_like(o_ref)


def _moe_group(h, gain, tok_of_pos, gate_of_pos, tile_expert, n_used, wg, wu, wd, tmg, tf):
    m, d = h.shape
    ne, _, ff = wg.shape
    n_tiles = tok_of_pos.shape[0]
    nk = ff // tf

    def wmap_in(t, k, texp, nused):
        return (texp[t], 0, jnp.where(t < nused[0], k, nk - 1))

    def wmap_out(t, k, texp, nused):
        return (texp[t], jnp.where(t < nused[0], k, nk - 1), 0)

    def tmap(t, k, texp, nused):
        return (t, 0)

    def cmap(t, k, texp, nused):
        return (0, 0)

    grid_spec = pltpu.PrefetchScalarGridSpec(
        num_scalar_prefetch=2, grid=(n_tiles, nk),
        in_specs=[pl.BlockSpec(memory_space=pl.ANY), pl.BlockSpec(memory_space=pl.ANY),
                  pl.BlockSpec((1, d), cmap), pl.BlockSpec((tmg, 1), tmap),
                  pl.BlockSpec((None, d, tf), wmap_in), pl.BlockSpec((None, d, tf), wmap_in),
                  pl.BlockSpec((None, tf, d), wmap_out)],
        out_specs=pl.BlockSpec((tmg, d), tmap),
        scratch_shapes=[pltpu.SMEM((1, tmg), jnp.int32), pltpu.VMEM((tmg, d), F32),
                        pltpu.VMEM((tmg, d), BF16), pltpu.VMEM((tmg, d), F32),
                        pltpu.SemaphoreType.DMA, pltpu.SemaphoreType.DMA])
    return pl.pallas_call(
        _moe_group_kernel, grid_spec=grid_spec,
        out_shape=jax.ShapeDtypeStruct((n_tiles * tmg, d), F32),
        compiler_params=_params("arbitrary", "arbitrary"), name="moe_group")(
            tile_expert, n_used, tok_of_pos, h, gain, gate_of_pos, wg, wu, wd)


def _moe_combine_ple_kernel(pos_hbm, ys_hbm, h_ref, p_ref, gn_ref, gw_ref, proj_ref, fn_ref, o_ref,
                            idx_smem, ybuf, sem_idx, sem_rows):
    tm = h_ref.shape[0]
    _gather_rows(pos_hbm, ys_hbm, idx_smem, ybuf, sem_idx, sem_rows, pl.program_id(0), 2 * tm)
    h = h_ref[...] + (ybuf[0:tm, :] + ybuf[tm:2 * tm, :])
    emb = jnp.dot(p_ref[...].astype(BF16), proj_ref[...], preferred_element_type=F32)
    hn = _rms(h, gn_ref[...]).astype(BF16)
    gate = jax.nn.sigmoid(jnp.dot(hn, gw_ref[...], preferred_element_type=F32))
    o_ref[...] = _rms(h + gate * emb, fn_ref[...])


def _moe_combine_ple(h, ys, pos_tiles, p, gate_norm, gate_w, proj, final_norm, tm):
    m, d = h.shape
    pd = p.shape[1]
    return pl.pallas_call(
        _moe_combine_ple_kernel, grid=(m // tm,),
        in_specs=[pl.BlockSpec(memory_space=pl.ANY), pl.BlockSpec(memory_space=pl.ANY),
                  pl.BlockSpec((tm, d), lambda i: (i, 0)), pl.BlockSpec((tm, pd), lambda i: (i, 0)),
                  pl.BlockSpec((1, d), lambda i: (0, 0)), pl.BlockSpec((d, d), lambda i: (0, 0)),
                  pl.BlockSpec((pd, d), lambda i: (0, 0)), pl.BlockSpec((1, d), lambda i: (0, 0))],
        out_specs=pl.BlockSpec((tm, d), lambda i: (i, 0)),
        out_shape=jax.ShapeDtypeStruct(h.shape, F32),
        scratch_shapes=[pltpu.SMEM((1, 2 * tm), jnp.int32), pltpu.VMEM((2 * tm, d), F32),
                        pltpu.SemaphoreType.DMA, pltpu.SemaphoreType.DMA],
        compiler_params=_params("arbitrary"), name="moe_combine_ple")(
            pos_tiles, ys, h, p, gate_norm, gate_w, proj, final_norm)


def _moe_dispatch_plan(sel, tmg, tmc):
    m = sel.shape[0]
    e_flat = sel[:, 0:2].astype(jnp.int32).reshape(-1)
    w_flat = sel[:, 2:4].reshape(-1)
    onehot = (e_flat[:, None] == jnp.arange(N_EXPERTS, dtype=jnp.int32)[None, :]).astype(jnp.int32)
    csum = jnp.cumsum(onehot, axis=0)
    rank = jnp.sum((csum - onehot) * onehot, axis=1)
    tiles_e = (csum[-1] + tmg - 1) // tmg
    tile_end = jnp.cumsum(tiles_e)
    pos = (tile_end - tiles_e)[e_flat] * tmg + rank
    n_tiles = (2 * m) // tmg + N_EXPERTS
    tok_of_pos = jnp.zeros((n_tiles * tmg,), jnp.int32).at[pos].set(jnp.arange(2 * m, dtype=jnp.int32) // 2)
    gate_of_pos = jnp.zeros((n_tiles * tmg,), F32).at[pos].set(w_flat)
    tile_expert = jnp.minimum(jnp.searchsorted(tile_end, jnp.arange(n_tiles, dtype=jnp.int32), side="right"),
                              N_EXPERTS - 1).astype(jnp.int32)
    pos_tiles = pos.reshape(m // tmc, tmc, 2).transpose(0, 2, 1).reshape(m // tmc, 1, 2 * tmc)
    return (tok_of_pos.reshape(n_tiles, 1, tmg), gate_of_pos.reshape(-1, 1), tile_expert,
            tile_end[-1:].astype(jnp.int32), pos_tiles.astype(jnp.int32))


def _row(v):
    return v.reshape(1, -1).astype(F32)


def _pad_cols(w, width):
    return jnp.pad(w, ((0, 0), (0, width - w.shape[1])))


def _tile(m, want):
    t = min(want, m)
    while m % t:
        t //= 2
    return t


def _even_layer(h, p_i, bsz, seq, mix_norm, w_in, conv_w, conv_b, wa, ba, wx, bx, lam,
                w_out, ffn_norm, ffn_wg, ffn_wu, ffn_wd, ple_proj, ple_gate_norm, ple_gate):
    m = h.shape[0]
    qw = A_HEADS * HEAD_DIM
    kw = A_KV_HEADS * HEAD_DIM
    iw = IDX_HEADS * IDX_DIM
    o = 0
    wq = w_in[:, o:o + qw]; o += qw
    wk = w_in[:, o:o + kw]; o += kw
    wv = w_in[:, o:o + kw]; o += kw
    wqi = w_in[:, o:o + iw]; o += iw
    wki = w_in[:, o:o + IDX_DIM]; o += IDX_DIM
    wwi = w_in[:, o:o + IDX_HEADS]; o += IDX_HEADS
    wxg = w_in[:, o:o + LRU_WIDTH]; o += LRU_WIDTH
    wxr = w_in[:, o:o + LRU_WIDTH]

    def dup_heads(wm):
        parts = []
        for g in range(wm.shape[1] // HEAD_DIM):
            blk = wm[:, g * HEAD_DIM:(g + 1) * HEAD_DIM]
            parts += [blk, blk]
        return jnp.concatenate(parts, axis=1)

    w_all = jnp.concatenate([wq, dup_heads(wk), dup_heads(wv), wqi, dup_heads(wki),
                             _pad_cols(wwi, V7X_LANES), wxg, wxr], axis=1).astype(BF16)
    sections = ((qw, True), (2 * kw, True), (2 * kw, False), (iw, True), (2 * IDX_DIM, True),
                (V7X_LANES, False), (LRU_WIDTH, False), (LRU_WIDTH, False))
    dtypes = (BF16, BF16, BF16, BF16, BF16, F32, F32, F32)
    tm = _tile(seq, 512)
    q, kk, vv, qi, kiki, wi, xg, xr = _in_proj(h, _row(mix_norm), w_all, sections, dtypes, tm,
                                               rope_tables=_rope_tables(seq), seq=seq)
    o_a = _dsa(q, qi, wi, kk, vv, kiki, bsz, seq)
    o_b = _lru(xr, xg, conv_w, _row(conv_b), _block_diag_pairs(wa), _row(ba),
               _block_diag_pairs(wx), _row(bx), _row(lam), bsz, seq)
    w_out_b = w_out.astype(BF16)
    h = _res_matmul(h, [o_a, o_b], [w_out_b[:qw], w_out_b[qw:]], _tile(m, 512))
    h = _ffn(h, _row(ffn_norm), ffn_wg.astype(BF16), ffn_wu.astype(BF16), ffn_wd.astype(BF16),
             _tile(m, 1024), 512)
    return _ple(h, p_i, _row(ple_gate_norm), ple_gate.astype(BF16), ple_proj.astype(BF16), _tile(m, 512))


def _odd_layer(h, p_i, bsz, seq, mix_norm, w_in, conv_w, conv_b, dt_bias, a_log, d_skip, gnorm, w_out,
               moe_norm, router, exp_wg, exp_wu, exp_wd, ple_proj, ple_gate_norm, ple_gate, final_norm):
    m = h.shape[0]
    inner = gnorm.shape[0]
    n_heads = dt_bias.shape[0]
    cch = conv_w.shape[1]
    w_z = w_in[:, :inner]
    w_xbc = w_in[:, inner:inner + cch]
    w_dt = _pad_cols(w_in[:, inner + cch:], V7X_LANES)
    w_all = jnp.concatenate([w_z, w_xbc, w_dt], axis=1).astype(BF16)
    sections = ((inner, False), (cch, False), (V7X_LANES, False))
    z, xbc, dt = _in_proj(h, _row(mix_norm), w_all, sections, (F32, F32, F32), _tile(m, 256))
    pad = V7X_LANES - n_heads
    y = _ssd(xbc, z, dt, conv_w, _row(conv_b), _row(jnp.pad(dt_bias, (0, pad))),
             _row(jnp.pad(a_log, (0, pad))), _row(jnp.repeat(d_skip, SSD_HEAD_DIM)), _row(gnorm),
             bsz, seq, inner, n_heads)
    h = _res_matmul(h, [y], [w_out.astype(BF16)], _tile(m, 512))
    sel = _router(h, _row(moe_norm), _pad_cols(router, V7X_LANES), _tile(m, 512))
    tmg = _tile(2 * m, 512)
    tmc = _tile(m, 256)
    tok_of_pos, gate_of_pos, tile_expert, n_used, pos_tiles = _moe_dispatch_plan(sel, tmg, tmc)
    ys = _moe_group(h, _row(moe_norm), tok_of_pos, gate_of_pos, tile_expert, n_used,
                    exp_wg.astype(BF16), exp_wu.astype(BF16), exp_wd.astype(BF16), tmg, 896)
    return _moe_combine_ple(h, ys, pos_tiles, p_i, _row(ple_gate_norm), ple_gate.astype(BF16),
                            ple_proj.astype(BF16), _row(final_norm), tmc)


def kernel(x, p, e_mix_norm, e_w_in, e_lru_conv_w, e_lru_conv_b, e_lru_wa, e_lru_ba, e_lru_wx, e_lru_bx, e_lru_lambda, e_w_out, e_ffn_norm, e_ffn_wg, e_ffn_wu, e_ffn_wd, e_ple_proj, e_ple_gate_norm, e_ple_gate, o_mix_norm, o_w_in, o_conv_w, o_conv_b, o_dt_bias, o_a_log, o_d_skip, o_gnorm, o_w_out, o_moe_norm, o_router, o_exp_wg, o_exp_wu, o_exp_wd, o_ple_proj, o_ple_gate_norm, o_ple_gate, final_norm):
    bsz, seq, d = x.shape
    m = bsz * seq
    h = x.reshape(m, d)
    pf = p.reshape(p.shape[0], m, p.shape[-1])
    h = _even_layer(h, pf[0], bsz, seq, e_mix_norm, e_w_in, e_lru_conv_w, e_lru_conv_b, e_lru_wa, e_lru_ba,
                    e_lru_wx, e_lru_bx, e_lru_lambda, e_w_out, e_ffn_norm, e_ffn_wg, e_ffn_wu, e_ffn_wd,
                    e_ple_proj, e_ple_gate_norm, e_ple_gate)
    h = _odd_layer(h, pf[1], bsz, seq, o_mix_norm, o_w_in, o_conv_w, o_conv_b, o_dt_bias, o_a_log, o_d_skip,
                   o_gnorm, o_w_out, o_moe_norm, o_router, o_exp_wg, o_exp_wu, o_exp_wd,
                   o_ple_proj, o_ple_gate_norm, o_ple_gate, final_norm)
    return h.reshape(bsz, seq, d)
```

```python
import functools
import math

import jax
import jax.numpy as jnp
from jax import lax
from jax.experimental import pallas as pl
from jax.experimental.pallas import tpu as pltpu

F32 = jnp.float32
BF16 = jnp.bfloat16

V7X_LANES = 128
V7X_SUBLANES = 8
V7X_VMEM_LIMIT_BYTES = 56 * 1024 * 1024

NORM_EPS = 1e-6
HEAD_DIM = 64
A_HEADS = 8
A_KV_HEADS = 2
ROPE_DIM = HEAD_DIM // 4
ROPE_THETA = 500000.0
IDX_HEADS = 4
IDX_DIM = 64
TOPK_MAX = 256
DSA_ROWS = 256
LRU_WIDTH = 512
LRU_BLOCKS = 8
CONV_W = 4
LRU_C = 8.0
SSD_HEAD_DIM = 64
SSD_GROUPS = 4
SSD_STATE = 128
SSD_CHUNK = 128
N_EXPERTS = 8
F32_MAX = float(jnp.finfo(jnp.float32).max)


def _params(*semantics):
    return pltpu.CompilerParams(dimension_semantics=semantics,
                                vmem_limit_bytes=V7X_VMEM_LIMIT_BYTES)


def _rms(x, g):
    ms = jnp.mean(x * x, axis=-1, keepdims=True)
    return x * lax.rsqrt(ms + NORM_EPS) * g


def _silu(x):
    return x * jax.nn.sigmoid(x)


def _softplus(x):
    return jnp.maximum(x, 0.0) + jnp.log1p(jnp.exp(-jnp.abs(x)))


def _row_shift(v, d, fill, row):
    return jnp.where(row >= d, pltpu.roll(v, d, 0), fill)


def _in_proj_kernel(*refs, sections, use_rope):
    if use_rope:
        x_ref, g_ref, w_ref, cos_ref, sa_ref, sb_ref = refs[:6]
        out_refs = refs[6:]
    else:
        x_ref, g_ref, w_ref = refs[:3]
        out_refs = refs[3:]
    xn = _rms(x_ref[...], g_ref[...]).astype(BF16)
    off = 0
    for out_ref, (width, rope) in zip(out_refs, sections):
        y = jnp.dot(xn, w_ref[:, off:off + width], preferred_element_type=F32)
        if rope:
            c, sa, sb = cos_ref[...], sa_ref[...], sb_ref[...]
            for ch in range(width // V7X_LANES):
                lo = ch * V7X_LANES
                yc = y[:, lo:lo + V7X_LANES]
                yr = (yc * c + pltpu.roll(yc, V7X_LANES - ROPE_DIM // 2, 1) * sa
                      + pltpu.roll(yc, ROPE_DIM // 2, 1) * sb)
                out_ref[:, lo:lo + V7X_LANES] = yr.astype(out_ref.dtype)
        else:
            out_ref[...] = y.astype(out_ref.dtype)
        off += width


def _in_proj(x, gain, w, sections, dtypes, tm, rope_tables=None, seq=None):
    m, d = x.shape
    n = w.shape[1]
    use_rope = rope_tables is not None
    in_specs = [pl.BlockSpec((tm, d), lambda i: (i, 0)),
                pl.BlockSpec((1, d), lambda i: (0, 0)),
                pl.BlockSpec((d, n), lambda i: (0, 0))]
    args = [x, gain, w]
    if use_rope:
        nper = seq // tm
        for t in rope_tables:
            in_specs.append(pl.BlockSpec((tm, V7X_LANES), lambda i: (i % nper, 0)))
            args.append(t)
    out_shape = [jax.ShapeDtypeStruct((m, wd), dt) for (wd, _), dt in zip(sections, dtypes)]
    out_specs = [pl.BlockSpec((tm, wd), lambda i: (i, 0)) for (wd, _) in sections]
    return pl.pallas_call(
        functools.partial(_in_proj_kernel, sections=sections, use_rope=use_rope),
        grid=(m // tm,), in_specs=in_specs, out_specs=out_specs, out_shape=out_shape,
        compiler_params=_params("parallel"), name="in_proj")(*args)


def _rope_tables(seq):
    half = ROPE_DIM // 2
    inv = ROPE_THETA ** (-jnp.arange(half, dtype=F32) / half)
    ang = jnp.arange(seq, dtype=F32)[:, None] * inv[None, :]
    cos, sin = jnp.cos(ang), jnp.sin(ang)
    ones = jnp.ones((seq, HEAD_DIM - ROPE_DIM), F32)
    zeros = jnp.zeros((seq, HEAD_DIM - ROPE_DIM), F32)
    zh = jnp.zeros((seq, half), F32)
    c = jnp.concatenate([cos, cos, ones], axis=1)
    sa = jnp.concatenate([-sin, zh, zeros], axis=1)
    sb = jnp.concatenate([zh, sin, zeros], axis=1)
    rep = V7X_LANES // HEAD_DIM
    return tuple(jnp.tile(t, (1, rep)) for t in (c, sa, sb))


def _dsa_body(q_ref, qi_ref, wi_ref, kk_ref, vv_ref, kiki_ref, o_ref, score_ref, bias_ref,
              *, sk, n_sel):
    t = DSA_ROWS
    j = pl.program_id(1)
    lane = lax.broadcasted_iota(jnp.int32, (t, V7X_LANES), 1)
    lo_half = lane < HEAD_DIM
    row = j * t + lax.broadcasted_iota(jnp.int32, (t, 1), 0)
    col = lax.broadcasted_iota(jnp.int32, (t, sk), 1)
    nt = (((1,), (1,)), ((), ()))

    kiki = kiki_ref[0:sk, :]
    wi = wi_ref[...] * (IDX_HEADS ** -0.5 * IDX_DIM ** -0.5)
    score = jnp.zeros((t, sk), F32)
    for h in range(IDX_HEADS):
        c = h // 2
        qc = qi_ref[:, c * V7X_LANES:(c + 1) * V7X_LANES]
        qm = jnp.where(lo_half if h % 2 == 0 else jnp.logical_not(lo_half), qc, jnp.zeros_like(qc))
        s = lax.dot_general(qm, kiki, nt, preferred_element_type=F32)
        score = score + wi[:, h:h + 1] * jnp.maximum(s, 0.0)
    score_ref[:, 0:sk] = jnp.where(col <= row, score, -jnp.inf)

    def search(i, u):
        u_try = u | lax.shift_left(jnp.int32(1), 31 - i)
        bits = jnp.where(u_try < 0, u_try ^ jnp.int32(-2 ** 31), ~u_try)
        cand = lax.bitcast_convert_type(bits, F32)
        cnt = jnp.sum(jnp.where(score_ref[:, 0:sk] >= cand, 1.0, 0.0), axis=1, keepdims=True)
        return jnp.where(cnt >= n_sel, u_try, u)

    u = lax.fori_loop(0, 32, search, jnp.zeros((t, 1), jnp.int32))
    bits = jnp.where(u < 0, u ^ jnp.int32(-2 ** 31), ~u)
    thr = lax.bitcast_convert_type(bits, F32)
    thr = jnp.where(row + 1 <= n_sel, -F32_MAX, thr)

    cnt_gt = jnp.sum(jnp.where(score_ref[:, 0:sk] > thr, 1.0, 0.0), axis=1, keepdims=True)
    need = n_sel - cnt_gt
    tri = (lax.broadcasted_iota(jnp.int32, (V7X_LANES, V7X_LANES), 0)
           <= lax.broadcasted_iota(jnp.int32, (V7X_LANES, V7X_LANES), 1))
    tri = jnp.where(tri, 1.0, 0.0).astype(BF16)
    carry = jnp.zeros((t, 1), F32)
    for c in range(sk // V7X_LANES):
        sc = score_ref[:, c * V7X_LANES:(c + 1) * V7X_LANES]
        eq = sc == thr
        pre = jnp.dot(jnp.where(eq, 1.0, 0.0).astype(BF16), tri, preferred_element_type=F32) + carry
        carry = pre[:, V7X_LANES - 1:V7X_LANES]
        tie = jnp.where(eq, jnp.where(pre <= need, 0.0, -jnp.inf), -jnp.inf)
        bias_ref[:, c * V7X_LANES:(c + 1) * V7X_LANES] = jnp.where(sc > thr, 0.0, tie)

    heads_per_chunk = V7X_LANES // HEAD_DIM
    groups = A_HEADS // A_KV_HEADS
    for c in range(A_HEADS // heads_per_chunk):
        g = (c * heads_per_chunk) // groups
        kk = kk_ref[0:sk, g * V7X_LANES:(g + 1) * V7X_LANES]
        vv = vv_ref[0:sk, g * V7X_LANES:(g + 1) * V7X_LANES]
        qc = q_ref[:, c * V7X_LANES:(c + 1) * V7X_LANES]
        outs = []
        for half in range(heads_per_chunk):
            qm = jnp.where(lo_half if half == 0 else jnp.logical_not(lo_half), qc, jnp.zeros_like(qc))
            lg = lax.dot_general(qm, kk, nt, preferred_element_type=F32) + bias_ref[:, 0:sk]
            mx = jnp.max(lg, axis=-1, keepdims=True)
            p = jnp.exp(lg - mx)
            den = jnp.sum(p, axis=-1, keepdims=True)
            outs.append(jnp.dot(p.astype(BF16), vv, preferred_element_type=F32) / den)
        o_ref[:, c * V7X_LANES:(c + 1) * V7X_LANES] = jnp.where(lo_half, outs[0], outs[1]).astype(o_ref.dtype)


def _dsa_kernel(q_ref, qi_ref, wi_ref, kk_ref, vv_ref, kiki_ref, o_ref, score_ref, bias_ref,
                *, classes, n_sel):
    j = pl.program_id(1)
    prev = 0
    for jmax, sk in classes:
        @pl.when(jnp.logical_and(j >= prev, j < jmax))
        def _(sk=sk):
            _dsa_body(q_ref, qi_ref, wi_ref, kk_ref, vv_ref, kiki_ref, o_ref, score_ref, bias_ref,
                      sk=sk, n_sel=n_sel)
        prev = jmax


def _dsa_classes(nblk):
    n_cls = min(4, nblk)
    bounds = sorted({-(-nblk * (i + 1) // n_cls) for i in range(n_cls)})
    return tuple((b, b * DSA_ROWS) for b in bounds)


def _dsa(q, qi, wi, kk, vv, kiki, bsz, seq):
    n_sel = min(TOPK_MAX, seq // 4)
    nblk = seq // DSA_ROWS
    t = DSA_ROWS

    def qmap(b, j):
        return (b * nblk + j, 0)

    def kmap(b, j):
        return (b, 0)

    return pl.pallas_call(
        functools.partial(_dsa_kernel, classes=_dsa_classes(nblk), n_sel=float(n_sel)),
        grid=(bsz, nblk),
        in_specs=[pl.BlockSpec((t, q.shape[1]), qmap),
                  pl.BlockSpec((t, qi.shape[1]), qmap),
                  pl.BlockSpec((t, wi.shape[1]), qmap),
                  pl.BlockSpec((seq, kk.shape[1]), kmap),
                  pl.BlockSpec((seq, vv.shape[1]), kmap),
                  pl.BlockSpec((seq, kiki.shape[1]), kmap)],
        out_specs=pl.BlockSpec((t, q.shape[1]), qmap),
        out_shape=jax.ShapeDtypeStruct(q.shape, BF16),
        scratch_shapes=[pltpu.VMEM((t, seq), F32), pltpu.VMEM((t, seq), F32)],
        compiler_params=_params("parallel", "arbitrary"), name="dsa")(q, qi, wi, kk, vv, kiki)


def _lru_kernel(xr_ref, xg_ref, cw_ref, cb_ref, wa_ref, ba_ref, wx_ref, bx_ref, lam_ref, o_ref):
    x = xr_ref[...]
    seq = x.shape[0]
    row = lax.broadcasted_iota(jnp.int32, x.shape, 0)
    w = cw_ref[...]
    xc = x * w[CONV_W - 1:CONV_W, :] + cb_ref[...]
    for d in range(1, CONV_W):
        xc = xc + _row_shift(x, d, 0.0, row) * w[CONV_W - 1 - d:CONV_W - d, :]
    xcb = xc.astype(BF16)
    gate_r = jax.nn.sigmoid(jnp.dot(xcb, wa_ref[...], preferred_element_type=F32) + ba_ref[...])
    gate_i = jax.nn.sigmoid(jnp.dot(xcb, wx_ref[...], preferred_element_type=F32) + bx_ref[...])
    log_a = -LRU_C * gate_r * _softplus(-lam_ref[...])
    a = jnp.exp(log_a)
    b = jnp.sqrt(-jnp.tanh(log_a) * (a * a + 1.0)) * (gate_i * xc)
    d = 1
    while d < seq:
        b = a * _row_shift(b, d, 0.0, row) + b
        a = a * _row_shift(a, d, 1.0, row)
        d *= 2
    o_ref[...] = (b * jax.nn.gelu(xg_ref[...], approximate=True)).astype(o_ref.dtype)


def _lru(xr, xg, cw, cb, wa_bd, ba, wx_bd, bx, lam, bsz, seq):
    nch = LRU_WIDTH // V7X_LANES

    def xmap(b, c):
        return (b, c)

    def pmap(b, c):
        return (0, c)

    def wmap(b, c):
        return (c, 0, 0)

    vec = pl.BlockSpec((1, V7X_LANES), pmap)
    blk = pl.BlockSpec((None, V7X_LANES, V7X_LANES), wmap)
    return pl.pallas_call(
        _lru_kernel, grid=(bsz, nch),
        in_specs=[pl.BlockSpec((seq, V7X_LANES), xmap), pl.BlockSpec((seq, V7X_LANES), xmap),
                  pl.BlockSpec((CONV_W, V7X_LANES), pmap), vec, blk, vec, blk, vec, vec],
        out_specs=pl.BlockSpec((seq, V7X_LANES), xmap),
        out_shape=jax.ShapeDtypeStruct(xr.shape, BF16),
        compiler_params=_params("parallel", "parallel"), name="rglru")(
            xr, xg, cw, cb, wa_bd, ba, wx_bd, bx, lam)


def _block_diag_pairs(w):
    n, c, _ = w.shape
    z = jnp.zeros((n // 2, c, c), w.dtype)
    top = jnp.concatenate([w[0::2], z], axis=2)
    bot = jnp.concatenate([z, w[1::2]], axis=2)
    return jnp.concatenate([top, bot], axis=1).astype(BF16)


def _ple_add(h, p_ref, gn_ref, gw_ref, proj_ref):
    emb = jnp.dot(p_ref[...].astype(BF16), proj_ref[...], preferred_element_type=F32)
    hn = _rms(h, gn_ref[...]).astype(BF16)
    gate = jax.nn.sigmoid(jnp.dot(hn, gw_ref[...], preferred_element_type=F32))
    return h + gate * emb


def _even_tail_kernel(h_ref, oa_ref, ob_ref, woa_ref, wob_ref, g_ref, wg_ref, wu_ref, wd_ref,
                      p_ref, gn_ref, gw_ref, proj_ref, o_ref, h1_ref, xn_ref, acc_ref):
    k = pl.program_id(1)

    @pl.when(k == 0)
    def _():
        h1 = (h_ref[...] + jnp.dot(oa_ref[...], woa_ref[...], preferred_element_type=F32)
              + jnp.dot(ob_ref[...], wob_ref[...], preferred_element_type=F32))
        h1_ref[...] = h1
        xn_ref[...] = _rms(h1, g_ref[...]).astype(BF16)
        acc_ref[...] = jnp.zeros_like(acc_ref)

    xn = xn_ref[...]
    gate = jnp.dot(xn, wg_ref[...], preferred_element_type=F32)
    up = jnp.dot(xn, wu_ref[...], preferred_element_type=F32)
    act = (_silu(gate) * up).astype(BF16)
    acc_ref[...] += jnp.dot(act, wd_ref[...], preferred_element_type=F32)

    @pl.when(k == pl.num_programs(1) - 1)
    def _():
        o_ref[...] = _ple_add(h1_ref[...] + acc_ref[...], p_ref, gn_ref, gw_ref, proj_ref)


def _even_tail(h, o_a, o_b, wo_a, wo_b, ffn_gain, wg, wu, wd, p, gate_norm, gate_w, proj, tm, tf):
    m, d = h.shape
    ff = wg.shape[1]

    def tmap(i, k):
        return (i, 0)

    def cmap(i, k):
        return (0, 0)

    def tok(width):
        return pl.BlockSpec((tm, width), tmap)

    def whole(a):
        return pl.BlockSpec(a.shape, cmap)

    return pl.pallas_call(
        _even_tail_kernel, grid=(m // tm, ff // tf),
        in_specs=[tok(d), tok(o_a.shape[1]), tok(o_b.shape[1]), whole(wo_a), whole(wo_b), whole(ffn_gain),
                  pl.BlockSpec((d, tf), lambda i, k: (0, k)), pl.BlockSpec((d, tf), lambda i, k: (0, k)),
                  pl.BlockSpec((tf, d), lambda i, k: (k, 0)),
                  tok(p.shape[1]), whole(gate_norm), whole(gate_w), whole(proj)],
        out_specs=tok(d),
        out_shape=jax.ShapeDtypeStruct(h.shape, F32),
        scratch_shapes=[pltpu.VMEM((tm, d), F32), pltpu.VMEM((tm, d), BF16), pltpu.VMEM((tm, d), F32)],
        compiler_params=_params("parallel", "arbitrary"), name="even_tail")(
            h, o_a, o_b, wo_a, wo_b, ffn_gain, wg, wu, wd, p, gate_norm, gate_w, proj)


def _expand_heads(v, n_heads):
    rows = v.shape[0]
    lane = lax.broadcasted_iota(jnp.int32, (rows, V7X_LANES), 1)
    per = V7X_LANES // SSD_HEAD_DIM
    chunks = []
    for c in range(n_heads // per):
        a = jnp.broadcast_to(v[:, per * c:per * c + 1], (rows, V7X_LANES))
        b = jnp.broadcast_to(v[:, per * c + 1:per * c + 2], (rows, V7X_LANES))
        chunks.append(jnp.where(lane < SSD_HEAD_DIM, a, b))
    return jnp.concatenate(chunks, axis=1)


def _ssd_kernel(xbc_ref, z_ref, dt_ref, cw_ref, cb_ref, dtb_ref, alog_ref, dskip_ref, gn_ref,
                y_ref, state_ref, halo_ref, *, inner, n_heads):
    ci = pl.program_id(1)
    cl = SSD_CHUNK
    gs = SSD_GROUPS * SSD_STATE
    hg = n_heads // SSD_GROUPS
    gw = inner // SSD_GROUPS

    @pl.when(ci == 0)
    def _():
        state_ref[...] = jnp.zeros_like(state_ref)
        halo_ref[...] = jnp.zeros_like(halo_ref)

    xraw = xbc_ref[...]
    halo = halo_ref[...]
    w = cw_ref[...]
    row8 = lax.broadcasted_iota(jnp.int32, halo.shape, 0)
    conv = xraw * w[CONV_W - 1:CONV_W, :] + cb_ref[...]
    for d in range(1, CONV_W):
        rolled = pltpu.roll(xraw, d, 0)
        top = jnp.where(row8 >= d, rolled[0:V7X_SUBLANES, :], pltpu.roll(halo, d, 0))
        shifted = jnp.concatenate([top, rolled[V7X_SUBLANES:, :]], axis=0)
        conv = conv + shifted * w[CONV_W - 1 - d:CONV_W - d, :]
    halo_ref[...] = xraw[cl - V7X_SUBLANES:cl, :]
    xc = _silu(conv)
    xs = xc[:, 0:inner]
    bm = xc[:, inner:inner + gs]
    cm = xc[:, inner + gs:inner + 2 * gs]

    dt = _softplus(dt_ref[...] + dtb_ref[...])
    da = dt * (-jnp.exp(alog_ref[...]))
    rowl = lax.broadcasted_iota(jnp.int32, da.shape, 0)
    cum = da
    d = 1
    while d < cl:
        cum = cum + _row_shift(cum, d, 0.0, rowl)
        d *= 2
    cum_t = cum.T
    dt_x = _expand_heads(dt, n_heads)
    cum_x = _expand_heads(cum, n_heads)
    last_x = cum_x[cl - 1:cl, :]
    xdt = xs * dt_x
    xdt_b = xdt.astype(BF16)
    tail_xdt = (jnp.exp(last_x - cum_x) * xdt).astype(BF16)
    exp_cum_x = jnp.exp(cum_x)
    state_decay = jnp.exp(last_x)

    tril = (lax.broadcasted_iota(jnp.int32, (cl, cl), 0) >= lax.broadcasted_iota(jnp.int32, (cl, cl), 1))
    lane = lax.broadcasted_iota(jnp.int32, (cl, V7X_LANES), 1)
    per = V7X_LANES // SSD_HEAD_DIM
    nt = (((1,), (1,)), ((), ()))
    dskip = dskip_ref[...]
    gn = gn_ref[...]
    for g in range(SSD_GROUPS):
        cg = cm[:, g * SSD_STATE:(g + 1) * SSD_STATE].astype(BF16)
        bg_f = bm[:, g * SSD_STATE:(g + 1) * SSD_STATE]
        cb = lax.dot_general(cg, bg_f.astype(BF16), nt, preferred_element_type=F32)
        st = state_ref[:, g * gw:(g + 1) * gw]
        y_off = jnp.dot(cg, st.astype(BF16), preferred_element_type=F32) * exp_cum_x[:, g * gw:(g + 1) * gw]
        diag = []
        for c in range(hg // per):
            pair = []
            for half in range(per):
                h = g * hg + c * per + half
                seg = cum[:, h:h + 1] - cum_t[h:h + 1, :]
                decay = jnp.exp(jnp.where(tril, seg, -jnp.inf))
                lo = g * gw + c * V7X_LANES
                pair.append(jnp.dot((cb * decay).astype(BF16), xdt_b[:, lo:lo + V7X_LANES],
                                    preferred_element_type=F32))
            diag.append(jnp.where(lane < SSD_HEAD_DIM, pair[0], pair[1]))
        y_diag = jnp.concatenate(diag, axis=1)
        new_st = st * state_decay[:, g * gw:(g + 1) * gw] + jnp.dot(
            bg_f.T.astype(BF16), tail_xdt[:, g * gw:(g + 1) * gw], preferred_element_type=F32)
        state_ref[:, g * gw:(g + 1) * gw] = new_st

        y = y_diag + y_off + dskip[:, g * gw:(g + 1) * gw] * xs[:, g * gw:(g + 1) * gw]
        y = y * _silu(z_ref[:, g * gw:(g + 1) * gw])
        y = y * lax.rsqrt(jnp.mean(y * y, axis=-1, keepdims=True) + NORM_EPS)
        y_ref[:, g * gw:(g + 1) * gw] = (y * gn[:, g * gw:(g + 1) * gw]).astype(y_ref.dtype)


def _ssd(xbc, z, dt, cw, cb, dtb, alog, dskip_x, gnorm, bsz, seq, inner, n_heads):
    nc = seq // SSD_CHUNK
    cl = SSD_CHUNK
    cch = xbc.shape[1]

    def xmap(b, c):
        return (b * nc + c, 0)

    def pmap(b, c):
        return (0, 0)

    return pl.pallas_call(
        functools.partial(_ssd_kernel, inner=inner, n_heads=n_heads), grid=(bsz, nc),
        in_specs=[pl.BlockSpec((cl, cch), xmap), pl.BlockSpec((cl, inner), xmap),
                  pl.BlockSpec((cl, V7X_LANES), xmap),
                  pl.BlockSpec((CONV_W, cch), pmap), pl.BlockSpec((1, cch), pmap),
                  pl.BlockSpec((1, V7X_LANES), pmap), pl.BlockSpec((1, V7X_LANES), pmap),
                  pl.BlockSpec((1, inner), pmap), pl.BlockSpec((1, inner), pmap)],
        out_specs=pl.BlockSpec((cl, inner), xmap),
        out_shape=jax.ShapeDtypeStruct((bsz * seq, inner), BF16),
        scratch_shapes=[pltpu.VMEM((SSD_STATE, inner), F32), pltpu.VMEM((V7X_SUBLANES, cch), F32)],
        compiler_params=_params("parallel", "arbitrary"), name="ssd")(
            xbc, z, dt, cw, cb, dtb, alog, dskip_x, gnorm)


def _router_kernel(h_ref, y_ref, wo_ref, g_ref, r_ref, hout_ref, gates_ref):
    h = h_ref[...] + jnp.dot(y_ref[...], wo_ref[...], preferred_element_type=F32)
    hout_ref[...] = h
    xn = _rms(h, g_ref[...])
    logits = jnp.dot(xn, r_ref[...], preferred_element_type=F32, precision=lax.Precision.HIGHEST)
    lane = lax.broadcasted_iota(jnp.int32, logits.shape, 1)
    logits = jnp.where(lane < N_EXPERTS, logits, -jnp.inf)
    m1 = jnp.max(logits, axis=-1, keepdims=True)
    i1 = jnp.min(jnp.where(logits == m1, lane, V7X_LANES), axis=-1, keepdims=True)
    rest = jnp.where(lane == i1, -jnp.inf, logits)
    m2 = jnp.max(rest, axis=-1, keepdims=True)
    i2 = jnp.min(jnp.where(rest == m2, lane, V7X_LANES), axis=-1, keepdims=True)
    e2 = jnp.exp(m2 - m1)
    w1 = 1.0 / (1.0 + e2)
    w2 = e2 / (1.0 + e2)
    gates_ref[...] = jnp.where(lane == 0, i1.astype(F32), jnp.where(
        lane == 1, i2.astype(F32), jnp.where(lane == 2, w1, jnp.where(lane == 3, w2, 0.0))))


def _router(h, y, w_out, gain, router_p, tm):
    m, d = h.shape
    return pl.pallas_call(
        _router_kernel, grid=(m // tm,),
        in_specs=[pl.BlockSpec((tm, d), lambda i: (i, 0)),
                  pl.BlockSpec((tm, y.shape[1]), lambda i: (i, 0)),
                  pl.BlockSpec(w_out.shape, lambda i: (0, 0)),
                  pl.BlockSpec((1, d), lambda i: (0, 0)),
                  pl.BlockSpec((d, V7X_LANES), lambda i: (0, 0))],
        out_specs=[pl.BlockSpec((tm, d), lambda i: (i, 0)),
                   pl.BlockSpec((tm, V7X_LANES), lambda i: (i, 0))],
        out_shape=[jax.ShapeDtypeStruct((m, d), F32), jax.ShapeDtypeStruct((m, V7X_LANES), F32)],
        compiler_params=_params("parallel"), name="router")(h, y, w_out, gain, router_p)


DMA_UNROLL = 8


def _row_dma_loop(n_rows, body):
    def trip(r8, carry):
        for u in range(DMA_UNROLL):
            body(r8 * DMA_UNROLL + u, u)
        return carry

    lax.fori_loop(0, n_rows // DMA_UNROLL, trip, 0)


def _moe_dispatch_kernel(pos_ref, h_hbm, xs_init_hbm, xs_hbm, sems):
    del xs_init_hbm
    i = pl.program_id(0)
    n_rows = pos_ref.shape[1]
    tm = n_rows // 2

    def copy(r, slot):
        tok = i * tm + lax.rem(r, tm)
        return pltpu.make_async_copy(h_hbm.at[pl.ds(tok, 1), :], xs_hbm.at[pl.ds(pos_ref[i, r], 1), :],
                                     sems.at[slot])

    def wait_all(slot):
        _row_dma_loop(n_rows, lambda r, u: pltpu.make_async_copy(
            h_hbm.at[pl.ds(0, 1), :], xs_hbm.at[pl.ds(0, 1), :], sems.at[slot]).wait())

    slot = lax.rem(i, 2)
    _row_dma_loop(n_rows, lambda r, u: copy(r, slot).start())

    @pl.when(i > 0)
    def _():
        wait_all(1 - slot)

    @pl.when(i == pl.num_programs(0) - 1)
    def _():
        wait_all(slot)


def _moe_dispatch(h, pos_tiles, n_slots):
    m, d = h.shape
    grid_spec = pltpu.PrefetchScalarGridSpec(
        num_scalar_prefetch=1, grid=(pos_tiles.shape[0],),
        in_specs=[pl.BlockSpec(memory_space=pl.ANY), pl.BlockSpec(memory_space=pl.ANY)],
        out_specs=pl.BlockSpec(memory_space=pl.ANY),
        scratch_shapes=[pltpu.SemaphoreType.DMA((2,))])
    return pl.pallas_call(
        _moe_dispatch_kernel, grid_spec=grid_spec,
        out_shape=jax.ShapeDtypeStruct((n_slots, d), F32),
        input_output_aliases={2: 0},
        compiler_params=_params("arbitrary"), name="moe_dispatch")(
            pos_tiles, h, jnp.zeros((n_slots, d), F32))


def _moe_group_kernel(texp_ref, nused_ref, x_ref, g_ref, wg_ref, wu_ref, wd_ref, o_ref, xn_ref, acc_ref):
    t = pl.program_id(0)
    k = pl.program_id(1)
    active = t < nused_ref[0]
    last = k == pl.num_programs(1) - 1

    @pl.when(jnp.logical_and(active, k == 0))
    def _():
        xn_ref[...] = _rms(x_ref[...], g_ref[...]).astype(BF16)
        acc_ref[...] = jnp.zeros_like(acc_ref)

    @pl.when(active)
    def _():
        xn = xn_ref[...]
        gate = jnp.dot(xn, wg_ref[...], preferred_element_type=F32)
        up = jnp.dot(xn, wu_ref[...], preferred_element_type=F32)
        act = (_silu(gate) * up).astype(BF16)
        acc_ref[...] += jnp.dot(act, wd_ref[...], preferred_element_type=F32)

    @pl.when(jnp.logical_and(active, last))
    def _():
        o_ref[...] = acc_ref[...]

    @pl.when(jnp.logical_and(jnp.logical_not(active), last))
    def _():
        o_ref[...] = jnp.zeros_like(o_ref)


def _moe_group(xs, gain, tile_expert, n_used, wg, wu, wd, tmg, tf):
    n_slots, d = xs.shape
    ne, _, ff = wg.shape
    n_tiles = n_slots // tmg
    nk = ff // tf

    def wmap_in(t, k, texp, nused):
        return (texp[t], 0, jnp.where(t < nused[0], k, nk - 1))

    def wmap_out(t, k, texp, nused):
        return (texp[t], jnp.where(t < nused[0], k, nk - 1), 0)

    def tmap(t, k, texp, nused):
        return (t, 0)

    def cmap(t, k, texp, nused):
        return (0, 0)

    grid_spec = pltpu.PrefetchScalarGridSpec(
        num_scalar_prefetch=2, grid=(n_tiles, nk),
        in_specs=[pl.BlockSpec((tmg, d), tmap), pl.BlockSpec((1, d), cmap),
                  pl.BlockSpec((None, d, tf), wmap_in), pl.BlockSpec((None, d, tf), wmap_in),
                  pl.BlockSpec((None, tf, d), wmap_out)],
        out_specs=pl.BlockSpec((tmg, d), tmap),
        scratch_shapes=[pltpu.VMEM((tmg, d), BF16), pltpu.VMEM((tmg, d), F32)])
    return pl.pallas_call(
        _moe_group_kernel, grid_spec=grid_spec,
        out_shape=jax.ShapeDtypeStruct((n_slots, d), F32),
        compiler_params=_params("arbitrary", "arbitrary"), name="moe_group")(
            tile_expert, n_used, xs, gain, wg, wu, wd)


def _moe_combine_ple_kernel(pos_ref, ys_hbm, h_ref, sel_ref, p_ref, gn_ref, gw_ref, proj_ref, fn_ref, o_ref,
                            ybuf, sems):
    i = pl.program_id(0)
    n_rows = pos_ref.shape[1]
    tm = n_rows // 2

    def copy(tile, r, slot):
        return pltpu.make_async_copy(ys_hbm.at[pl.ds(pos_ref[tile, r], 1), :],
                                     ybuf.at[slot, pl.ds(r, 1), :], sems.at[slot])

    def fetch(tile, slot):
        _row_dma_loop(n_rows, lambda r, u: copy(tile, r, slot).start(priority=u % 2))

    slot = lax.rem(i, 2)

    @pl.when(i == 0)
    def _():
        fetch(i, slot)

    @pl.when(i + 1 < pl.num_programs(0))
    def _():
        fetch(i + 1, 1 - slot)

    _row_dma_loop(n_rows, lambda r, u: copy(i, r, slot).wait())
    sel = sel_ref[...]
    y = ybuf[slot]
    h = h_ref[...] + (sel[:, 2:3] * y[0:tm, :] + sel[:, 3:4] * y[tm:n_rows, :])
    o_ref[...] = _rms(_ple_add(h, p_ref, gn_ref, gw_ref, proj_ref), fn_ref[...])


def _moe_combine_ple(h, ys, pos_tiles, sel, p, gate_norm, gate_w, proj, final_norm):
    m, d = h.shape
    pd = p.shape[1]
    n_tiles, n_rows = pos_tiles.shape
    tm = n_rows // 2

    def tmap(i, pos):
        return (i, 0)

    def cmap(i, pos):
        return (0, 0)

    grid_spec = pltpu.PrefetchScalarGridSpec(
        num_scalar_prefetch=1, grid=(n_tiles,),
        in_specs=[pl.BlockSpec(memory_space=pl.ANY),
                  pl.BlockSpec((tm, d), tmap), pl.BlockSpec((tm, V7X_LANES), tmap), pl.BlockSpec((tm, pd), tmap),
                  pl.BlockSpec((1, d), cmap), pl.BlockSpec((d, d), cmap),
                  pl.BlockSpec((pd, d), cmap), pl.BlockSpec((1, d), cmap)],
        out_specs=pl.BlockSpec((tm, d), tmap),
        scratch_shapes=[pltpu.VMEM((2, n_rows, d), F32), pltpu.SemaphoreType.DMA((2,))])
    return pl.pallas_call(
        _moe_combine_ple_kernel, grid_spec=grid_spec,
        out_shape=jax.ShapeDtypeStruct(h.shape, F32),
        compiler_params=_params("arbitrary"), name="moe_combine_ple")(
            pos_tiles, ys, h, sel, p, gate_norm, gate_w, proj, final_norm)


def _moe_dispatch_plan(sel, tmg, tmc):
    m = sel.shape[0]
    e_flat = sel[:, 0:2].astype(jnp.int32).reshape(-1)
    onehot = (e_flat[:, None] == jnp.arange(N_EXPERTS, dtype=jnp.int32)[None, :]).astype(jnp.int32)
    csum = jnp.cumsum(onehot, axis=0)
    rank = jnp.sum((csum - onehot) * onehot, axis=1)
    tiles_e = (csum[-1] + tmg - 1) // tmg
    tile_end = jnp.cumsum(tiles_e)
    pos = (tile_end - tiles_e)[e_flat] * tmg + rank
    n_tiles = (2 * m) // tmg + N_EXPERTS
    tile_expert = jnp.minimum(jnp.searchsorted(tile_end, jnp.arange(n_tiles, dtype=jnp.int32), side="right"),
                              N_EXPERTS - 1).astype(jnp.int32)
    pos_tiles = pos.reshape(m // tmc, tmc, 2).transpose(0, 2, 1).reshape(m // tmc, 2 * tmc)
    return pos_tiles.astype(jnp.int32), tile_expert, tile_end[-1:].astype(jnp.int32), n_tiles * tmg


def _row(v):
    return v.reshape(1, -1).astype(F32)


def _pad_cols(w, width):
    return jnp.pad(w, ((0, 0), (0, width - w.shape[1])))


def _tile(m, want):
    t = min(want, m)
    while m % t:
        t //= 2
    return t


def _even_layer(h, p_i, bsz, seq, mix_norm, w_in, conv_w, conv_b, wa, ba, wx, bx, lam,
                w_out, ffn_norm, ffn_wg, ffn_wu, ffn_wd, ple_proj, ple_gate_norm, ple_gate):
    m = h.shape[0]
    qw = A_HEADS * HEAD_DIM
    kw = A_KV_HEADS * HEAD_DIM
    iw = IDX_HEADS * IDX_DIM
    o = 0
    wq = w_in[:, o:o + qw] * HEAD_DIM ** -0.5; o += qw
    wk = w_in[:, o:o + kw]; o += kw
    wv = w_in[:, o:o + kw]; o += kw
    wqi = w_in[:, o:o + iw]; o += iw
    wki = w_in[:, o:o + IDX_DIM]; o += IDX_DIM
    wwi = w_in[:, o:o + IDX_HEADS]; o += IDX_HEADS
    wxg = w_in[:, o:o + LRU_WIDTH]; o += LRU_WIDTH
    wxr = w_in[:, o:o + LRU_WIDTH]

    def dup_heads(wm):
        parts = []
        for g in range(wm.shape[1] // HEAD_DIM):
            blk = wm[:, g * HEAD_DIM:(g + 1) * HEAD_DIM]
            parts += [blk, blk]
        return jnp.concatenate(parts, axis=1)

    w_all = jnp.concatenate([wq, dup_heads(wk), dup_heads(wv), wqi, dup_heads(wki),
                             _pad_cols(wwi, V7X_LANES), wxg, wxr], axis=1).astype(BF16)
    sections = ((qw, True), (2 * kw, True), (2 * kw, False), (iw, True), (2 * IDX_DIM, True),
                (V7X_LANES, False), (LRU_WIDTH, False), (LRU_WIDTH, False))
    dtypes = (BF16, BF16, BF16, BF16, BF16, F32, F32, F32)
    tm = _tile(seq, 512)
    q, kk, vv, qi, kiki, wi, xg, xr = _in_proj(h, _row(mix_norm), w_all, sections, dtypes, tm,
                                               rope_tables=_rope_tables(seq), seq=seq)
    o_a = _dsa(q, qi, wi, kk, vv, kiki, bsz, seq)
    o_b = _lru(xr, xg, conv_w, _row(conv_b), _block_diag_pairs(wa), _row(ba),
               _block_diag_pairs(wx), _row(bx), _row(lam), bsz, seq)
    w_out_b = w_out.astype(BF16)
    return _even_tail(h, o_a, o_b, w_out_b[:qw], w_out_b[qw:], _row(ffn_norm),
                      ffn_wg.astype(BF16), ffn_wu.astype(BF16), ffn_wd.astype(BF16),
                      p_i, _row(ple_gate_norm), ple_gate.astype(BF16), ple_proj.astype(BF16),
                      _tile(m, 512), 896)


def _odd_layer(h, p_i, bsz, seq, mix_norm, w_in, conv_w, conv_b, dt_bias, a_log, d_skip, gnorm, w_out,
               moe_norm, router, exp_wg, exp_wu, exp_wd, ple_proj, ple_gate_norm, ple_gate, final_norm):
    m = h.shape[0]
    inner = gnorm.shape[0]
    n_heads = dt_bias.shape[0]
    cch = conv_w.shape[1]
    w_z = w_in[:, :inner]
    w_xbc = w_in[:, inner:inner + cch]
    w_dt = _pad_cols(w_in[:, inner + cch:], V7X_LANES)
    w_all = jnp.concatenate([w_z, w_xbc, w_dt], axis=1).astype(BF16)
    sections = ((inner, False), (cch, False), (V7X_LANES, False))
    z, xbc, dt = _in_proj(h, _row(mix_norm), w_all, sections, (F32, F32, F32), _tile(m, 256))
    pad = V7X_LANES - n_heads
    y = _ssd(xbc, z, dt, conv_w, _row(conv_b), _row(jnp.pad(dt_bias, (0, pad))),
             _row(jnp.pad(a_log, (0, pad))), _row(jnp.repeat(d_skip, SSD_HEAD_DIM)), _row(gnorm),
             bsz, seq, inner, n_heads)
    h, sel = _router(h, y, w_out.astype(BF16), _row(moe_norm), _pad_cols(router, V7X_LANES), _tile(m, 512))
    tmg = _tile(2 * m, 512)
    tmc = _tile(m, 256)
    pos_tiles, tile_expert, n_used, n_slots = _moe_dispatch_plan(sel, tmg, tmc)
    xs = _moe_dispatch(h, pos_tiles, n_slots)
    ys = _moe_group(xs, _row(moe_norm), tile_expert, n_used,
                    exp_wg.astype(BF16), exp_wu.astype(BF16), exp_wd.astype(BF16), tmg, 896)
    return _moe_combine_ple(h, ys, pos_tiles, sel, p_i, _row(ple_gate_norm), ple_gate.astype(BF16),
                            ple_proj.astype(BF16), _row(final_norm))


def kernel(x, p, e_mix_norm, e_w_in, e_lru_conv_w, e_lru_conv_b, e_lru_wa, e_lru_ba, e_lru_wx, e_lru_bx, e_lru_lambda, e_w_out, e_ffn_norm, e_ffn_wg, e_ffn_wu, e_ffn_wd, e_ple_proj, e_ple_gate_norm, e_ple_gate, o_mix_norm, o_w_in, o_conv_w, o_conv_b, o_dt_bias, o_a_log, o_d_skip, o_gnorm, o_w_out, o_moe_norm, o_router, o_exp_wg, o_exp_wu, o_exp_wd, o_ple_proj, o_ple_gate_norm, o_ple_gate, final_norm):
    bsz, seq, d = x.shape
    m = bsz * seq
    h = x.reshape(m, d)
    pf = p.reshape(p.shape[0], m, p.shape[-1])
    h = _even_layer(h, pf[0], bsz, seq, e_mix_norm, e_w_in, e_lru_conv_w, e_lru_conv_b, e_lru_wa, e_lru_ba,
                    e_lru_wx, e_lru_bx, e_lru_lambda, e_w_out, e_ffn_norm, e_ffn_wg, e_ffn_wu, e_ffn_wd,
                    e_ple_proj, e_ple_gate_norm, e_ple_gate)
    h = _odd_layer(h, pf[1], bsz, seq, o_mix_norm, o_w_in, o_conv_w, o_conv_b, o_dt_bias, o_a_log, o_d_skip,
                   o_gnorm, o_w_out, o_moe_norm, o_router, o_exp_wg, o_exp_wu, o_exp_wd,
                   o_ple_proj, o_ple_gate_norm, o_ple_gate, final_norm)
    return h.reshape(bsz, seq, d)
```

```python
import functools
import math

import jax
import jax.numpy as jnp
from jax import lax
from jax.experimental import pallas as pl
from jax.experimental.pallas import tpu as pltpu

F32 = jnp.float32
BF16 = jnp.bfloat16

V7X_LANES = 128
V7X_SUBLANES = 8
V7X_VMEM_LIMIT_BYTES = 56 * 1024 * 1024

NORM_EPS = 1e-6
HEAD_DIM = 64
A_HEADS = 8
A_KV_HEADS = 2
ROPE_DIM = HEAD_DIM // 4
ROPE_THETA = 500000.0
IDX_HEADS = 4
IDX_DIM = 64
TOPK_MAX = 256
DSA_ROWS = 256
LRU_WIDTH = 512
LRU_BLOCKS = 8
CONV_W = 4
LRU_C = 8.0
SSD_HEAD_DIM = 64
SSD_GROUPS = 4
SSD_STATE = 128
SSD_CHUNK = 128
N_EXPERTS = 8
F32_MAX = float(jnp.finfo(jnp.float32).max)


def _params(*semantics):
    return pltpu.CompilerParams(dimension_semantics=semantics,
                                vmem_limit_bytes=V7X_VMEM_LIMIT_BYTES)


def _rms(x, g):
    ms = jnp.mean(x * x, axis=-1, keepdims=True)
    return x * lax.rsqrt(ms + NORM_EPS) * g


def _silu(x):
    return x * jax.nn.sigmoid(x)


def _softplus(x):
    return jnp.maximum(x, 0.0) + jnp.log1p(jnp.exp(-jnp.abs(x)))


def _row_shift(v, d, fill, row):
    return jnp.where(row >= d, pltpu.roll(v, d, 0), fill)


def _in_proj_kernel(*refs, sections, use_rope):
    if use_rope:
        x_ref, g_ref, w_ref, cos_ref, sa_ref, sb_ref = refs[:6]
        out_refs = refs[6:]
    else:
        x_ref, g_ref, w_ref = refs[:3]
        out_refs = refs[3:]
    xn = _rms(x_ref[...], g_ref[...]).astype(BF16)
    off = 0
    for out_ref, (width, rope) in zip(out_refs, sections):
        y = jnp.dot(xn, w_ref[:, off:off + width], preferred_element_type=F32)
        if rope:
            c, sa, sb = cos_ref[...], sa_ref[...], sb_ref[...]
            for ch in range(width // V7X_LANES):
                lo = ch * V7X_LANES
                yc = y[:, lo:lo + V7X_LANES]
                yr = (yc * c + pltpu.roll(yc, V7X_LANES - ROPE_DIM // 2, 1) * sa
                      + pltpu.roll(yc, ROPE_DIM // 2, 1) * sb)
                out_ref[:, lo:lo + V7X_LANES] = yr.astype(out_ref.dtype)
        else:
            out_ref[...] = y.astype(out_ref.dtype)
        off += width


def _in_proj(x, gain, w, sections, dtypes, tm, rope_tables=None, seq=None):
    m, d = x.shape
    n = w.shape[1]
    use_rope = rope_tables is not None
    in_specs = [pl.BlockSpec((tm, d), lambda i: (i, 0)),
                pl.BlockSpec((1, d), lambda i: (0, 0)),
                pl.BlockSpec((d, n), lambda i: (0, 0))]
    args = [x, gain, w]
    if use_rope:
        nper = seq // tm
        for t in rope_tables:
            in_specs.append(pl.BlockSpec((tm, V7X_LANES), lambda i: (i % nper, 0)))
            args.append(t)
    out_shape = [jax.ShapeDtypeStruct((m, wd), dt) for (wd, _), dt in zip(sections, dtypes)]
    out_specs = [pl.BlockSpec((tm, wd), lambda i: (i, 0)) for (wd, _) in sections]
    return pl.pallas_call(
        functools.partial(_in_proj_kernel, sections=sections, use_rope=use_rope),
        grid=(m // tm,), in_specs=in_specs, out_specs=out_specs, out_shape=out_shape,
        compiler_params=_params("parallel"), name="in_proj")(*args)


def _rope_tables(seq):
    half = ROPE_DIM // 2
    inv = ROPE_THETA ** (-jnp.arange(half, dtype=F32) / half)
    ang = jnp.arange(seq, dtype=F32)[:, None] * inv[None, :]
    cos, sin = jnp.cos(ang), jnp.sin(ang)
    ones = jnp.ones((seq, HEAD_DIM - ROPE_DIM), F32)
    zeros = jnp.zeros((seq, HEAD_DIM - ROPE_DIM), F32)
    zh = jnp.zeros((seq, half), F32)
    c = jnp.concatenate([cos, cos, ones], axis=1)
    sa = jnp.concatenate([-sin, zh, zeros], axis=1)
    sb = jnp.concatenate([zh, sin, zeros], axis=1)
    rep = V7X_LANES // HEAD_DIM
    return tuple(jnp.tile(t, (1, rep)) for t in (c, sa, sb))


def _dsa_body(q_ref, qi_ref, wi_ref, kk_ref, vv_ref, kiki_ref, o_ref, score_ref, bias_ref,
              *, sk, n_sel):
    t = DSA_ROWS
    j = pl.program_id(1)
    lane = lax.broadcasted_iota(jnp.int32, (t, V7X_LANES), 1)
    lo_half = lane < HEAD_DIM
    row = j * t + lax.broadcasted_iota(jnp.int32, (t, 1), 0)
    col = lax.broadcasted_iota(jnp.int32, (t, sk), 1)
    nt = (((1,), (1,)), ((), ()))

    kiki = kiki_ref[0:sk, :]
    wi = wi_ref[...] * (IDX_HEADS ** -0.5 * IDX_DIM ** -0.5)
    score = jnp.zeros((t, sk), F32)
    for h in range(IDX_HEADS):
        c = h // 2
        qc = qi_ref[:, c * V7X_LANES:(c + 1) * V7X_LANES]
        qm = jnp.where(lo_half if h % 2 == 0 else jnp.logical_not(lo_half), qc, jnp.zeros_like(qc))
        s = lax.dot_general(qm, kiki, nt, preferred_element_type=F32)
        score = score + wi[:, h:h + 1] * jnp.maximum(s, 0.0)
    score_ref[:, 0:sk] = jnp.where(col <= row, score, -jnp.inf)

    def search(i, u):
        u_try = u | lax.shift_left(jnp.int32(1), 31 - i)
        bits = jnp.where(u_try < 0, u_try ^ jnp.int32(-2 ** 31), ~u_try)
        cand = lax.bitcast_convert_type(bits, F32)
        cnt = jnp.sum(jnp.where(score_ref[:, 0:sk] >= cand, 1.0, 0.0), axis=1, keepdims=True)
        return jnp.where(cnt >= n_sel, u_try, u)

    u = lax.fori_loop(0, 32, search, jnp.zeros((t, 1), jnp.int32))
    bits = jnp.where(u < 0, u ^ jnp.int32(-2 ** 31), ~u)
    thr = lax.bitcast_convert_type(bits, F32)
    thr = jnp.where(row + 1 <= n_sel, -F32_MAX, thr)

    cnt_gt = jnp.sum(jnp.where(score_ref[:, 0:sk] > thr, 1.0, 0.0), axis=1, keepdims=True)
    need = n_sel - cnt_gt
    tri = (lax.broadcasted_iota(jnp.int32, (V7X_LANES, V7X_LANES), 0)
           <= lax.broadcasted_iota(jnp.int32, (V7X_LANES, V7X_LANES), 1))
    tri = jnp.where(tri, 1.0, 0.0).astype(BF16)
    carry = jnp.zeros((t, 1), F32)
    for c in range(sk // V7X_LANES):
        sc = score_ref[:, c * V7X_LANES:(c + 1) * V7X_LANES]
        eq = sc == thr
        pre = jnp.dot(jnp.where(eq, 1.0, 0.0).astype(BF16), tri, preferred_element_type=F32) + carry
        carry = pre[:, V7X_LANES - 1:V7X_LANES]
        tie = jnp.where(eq, jnp.where(pre <= need, 0.0, -jnp.inf), -jnp.inf)
        bias_ref[:, c * V7X_LANES:(c + 1) * V7X_LANES] = jnp.where(sc > thr, 0.0, tie)

    heads_per_chunk = V7X_LANES // HEAD_DIM
    groups = A_HEADS // A_KV_HEADS
    for c in range(A_HEADS // heads_per_chunk):
        g = (c * heads_per_chunk) // groups
        kk = kk_ref[0:sk, g * V7X_LANES:(g + 1) * V7X_LANES]
        vv = vv_ref[0:sk, g * V7X_LANES:(g + 1) * V7X_LANES]
        qc = q_ref[:, c * V7X_LANES:(c + 1) * V7X_LANES]
        outs = []
        for half in range(heads_per_chunk):
            qm = jnp.where(lo_half if half == 0 else jnp.logical_not(lo_half), qc, jnp.zeros_like(qc))
            lg = lax.dot_general(qm, kk, nt, preferred_element_type=F32) + bias_ref[:, 0:sk]
            mx = jnp.max(lg, axis=-1, keepdims=True)
            p = jnp.exp(lg - mx)
            den = jnp.sum(p, axis=-1, keepdims=True)
            outs.append(jnp.dot(p.astype(BF16), vv, preferred_element_type=F32) / den)
        o_ref[:, c * V7X_LANES:(c + 1) * V7X_LANES] = jnp.where(lo_half, outs[0], outs[1]).astype(o_ref.dtype)


def _dsa_kernel(q_ref, qi_ref, wi_ref, kk_ref, vv_ref, kiki_ref, o_ref, score_ref, bias_ref,
                *, classes, n_sel):
    j = pl.program_id(1)
    prev = 0
    for jmax, sk in classes:
        @pl.when(jnp.logical_and(j >= prev, j < jmax))
        def _(sk=sk):
            _dsa_body(q_ref, qi_ref, wi_ref, kk_ref, vv_ref, kiki_ref, o_ref, score_ref, bias_ref,
                      sk=sk, n_sel=n_sel)
        prev = jmax


def _dsa_classes(nblk):
    n_cls = min(4, nblk)
    bounds = sorted({-(-nblk * (i + 1) // n_cls) for i in range(n_cls)})
    return tuple((b, b * DSA_ROWS) for b in bounds)


def _dsa(q, qi, wi, kk, vv, kiki, bsz, seq):
    n_sel = min(TOPK_MAX, seq // 4)
    nblk = seq // DSA_ROWS
    t = DSA_ROWS

    def qmap(b, j):
        return (b * nblk + j, 0)

    def kmap(b, j):
        return (b, 0)

    return pl.pallas_call(
        functools.partial(_dsa_kernel, classes=_dsa_classes(nblk), n_sel=float(n_sel)),
        grid=(bsz, nblk),
        in_specs=[pl.BlockSpec((t, q.shape[1]), qmap),
                  pl.BlockSpec((t, qi.shape[1]), qmap),
                  pl.BlockSpec((t, wi.shape[1]), qmap),
                  pl.BlockSpec((seq, kk.shape[1]), kmap),
                  pl.BlockSpec((seq, vv.shape[1]), kmap),
                  pl.BlockSpec((seq, kiki.shape[1]), kmap)],
        out_specs=pl.BlockSpec((t, q.shape[1]), qmap),
        out_shape=jax.ShapeDtypeStruct(q.shape, BF16),
        scratch_shapes=[pltpu.VMEM((t, seq), F32), pltpu.VMEM((t, seq), F32)],
        compiler_params=_params("parallel", "arbitrary"), name="dsa")(q, qi, wi, kk, vv, kiki)


def _lru_kernel(xr_ref, xg_ref, cw_ref, cb_ref, wa_ref, ba_ref, wx_ref, bx_ref, lam_ref, o_ref):
    x = xr_ref[...]
    seq = x.shape[0]
    row = lax.broadcasted_iota(jnp.int32, x.shape, 0)
    w = cw_ref[...]
    xc = x * w[CONV_W - 1:CONV_W, :] + cb_ref[...]
    for d in range(1, CONV_W):
        xc = xc + _row_shift(x, d, 0.0, row) * w[CONV_W - 1 - d:CONV_W - d, :]
    xcb = xc.astype(BF16)
    gate_r = jax.nn.sigmoid(jnp.dot(xcb, wa_ref[...], preferred_element_type=F32) + ba_ref[...])
    gate_i = jax.nn.sigmoid(jnp.dot(xcb, wx_ref[...], preferred_element_type=F32) + bx_ref[...])
    log_a = -LRU_C * gate_r * _softplus(-lam_ref[...])
    a = jnp.exp(log_a)
    b = jnp.sqrt(-jnp.tanh(log_a) * (a * a + 1.0)) * (gate_i * xc)
    d = 1
    while d < seq:
        b = a * _row_shift(b, d, 0.0, row) + b
        a = a * _row_shift(a, d, 1.0, row)
        d *= 2
    o_ref[...] = (b * jax.nn.gelu(xg_ref[...], approximate=True)).astype(o_ref.dtype)


def _lru(xr, xg, cw, cb, wa_bd, ba, wx_bd, bx, lam, bsz, seq):
    nch = LRU_WIDTH // V7X_LANES

    def xmap(b, c):
        return (b, c)

    def pmap(b, c):
        return (0, c)

    def wmap(b, c):
        return (c, 0, 0)

    vec = pl.BlockSpec((1, V7X_LANES), pmap)
    blk = pl.BlockSpec((None, V7X_LANES, V7X_LANES), wmap)
    return pl.pallas_call(
        _lru_kernel, grid=(bsz, nch),
        in_specs=[pl.BlockSpec((seq, V7X_LANES), xmap), pl.BlockSpec((seq, V7X_LANES), xmap),
                  pl.BlockSpec((CONV_W, V7X_LANES), pmap), vec, blk, vec, blk, vec, vec],
        out_specs=pl.BlockSpec((seq, V7X_LANES), xmap),
        out_shape=jax.ShapeDtypeStruct(xr.shape, BF16),
        compiler_params=_params("parallel", "parallel"), name="rglru")(
            xr, xg, cw, cb, wa_bd, ba, wx_bd, bx, lam)


def _block_diag_pairs(w):
    n, c, _ = w.shape
    z = jnp.zeros((n // 2, c, c), w.dtype)
    top = jnp.concatenate([w[0::2], z], axis=2)
    bot = jnp.concatenate([z, w[1::2]], axis=2)
    return jnp.concatenate([top, bot], axis=1).astype(BF16)


def _res_matmul_kernel(*refs, n_in):
    h_ref = refs[0]
    o_ref = refs[-1]
    acc = h_ref[...]
    for i in range(n_in):
        acc = acc + jnp.dot(refs[1 + 2 * i][...], refs[2 + 2 * i][...], preferred_element_type=F32)
    o_ref[...] = acc


def _res_matmul(h, xs, ws, tm):
    m, d = h.shape
    in_specs = [pl.BlockSpec((tm, d), lambda i: (i, 0))]
    args = [h]
    for x, w in zip(xs, ws):
        in_specs.append(pl.BlockSpec((tm, x.shape[1]), lambda i: (i, 0)))
        in_specs.append(pl.BlockSpec(w.shape, lambda i: (0, 0)))
        args += [x, w]
    return pl.pallas_call(
        functools.partial(_res_matmul_kernel, n_in=len(xs)), grid=(m // tm,),
        in_specs=in_specs, out_specs=pl.BlockSpec((tm, d), lambda i: (i, 0)),
        out_shape=jax.ShapeDtypeStruct(h.shape, F32),
        compiler_params=_params("parallel"), name="res_matmul")(*args)


def _ffn_kernel(h_ref, g_ref, wg_ref, wu_ref, wd_ref, o_ref, xn_ref, acc_ref):
    k = pl.program_id(1)

    @pl.when(k == 0)
    def _():
        xn_ref[...] = _rms(h_ref[...], g_ref[...]).astype(BF16)
        acc_ref[...] = jnp.zeros_like(acc_ref)

    xn = xn_ref[...]
    gate = jnp.dot(xn, wg_ref[...], preferred_element_type=F32)
    up = jnp.dot(xn, wu_ref[...], preferred_element_type=F32)
    act = (_silu(gate) * up).astype(BF16)
    acc_ref[...] += jnp.dot(act, wd_ref[...], preferred_element_type=F32)

    @pl.when(k == pl.num_programs(1) - 1)
    def _():
        o_ref[...] = h_ref[...] + acc_ref[...]


def _ffn(h, gain, wg, wu, wd, tm, tf):
    m, d = h.shape
    ff = wg.shape[1]
    return pl.pallas_call(
        _ffn_kernel, grid=(m // tm, ff // tf),
        in_specs=[pl.BlockSpec((tm, d), lambda i, k: (i, 0)),
                  pl.BlockSpec((1, d), lambda i, k: (0, 0)),
                  pl.BlockSpec((d, tf), lambda i, k: (0, k)),
                  pl.BlockSpec((d, tf), lambda i, k: (0, k)),
                  pl.BlockSpec((tf, d), lambda i, k: (k, 0))],
        out_specs=pl.BlockSpec((tm, d), lambda i, k: (i, 0)),
        out_shape=jax.ShapeDtypeStruct(h.shape, F32),
        scratch_shapes=[pltpu.VMEM((tm, d), BF16), pltpu.VMEM((tm, d), F32)],
        compiler_params=_params("parallel", "arbitrary"), name="ffn")(h, gain, wg, wu, wd)


def _ple_add(h, p_ref, gn_ref, gw_ref, proj_ref):
    emb = jnp.dot(p_ref[...].astype(BF16), proj_ref[...], preferred_element_type=F32)
    hn = _rms(h, gn_ref[...]).astype(BF16)
    gate = jax.nn.sigmoid(jnp.dot(hn, gw_ref[...], preferred_element_type=F32))
    return h + gate * emb


def _ple_kernel(h_ref, p_ref, gn_ref, gw_ref, proj_ref, o_ref):
    o_ref[...] = _ple_add(h_ref[...], p_ref, gn_ref, gw_ref, proj_ref)


def _ple(h, p, gate_norm, gate_w, proj, tm):
    m, d = h.shape
    pd = p.shape[1]
    return pl.pallas_call(
        _ple_kernel, grid=(m // tm,),
        in_specs=[pl.BlockSpec((tm, d), lambda i: (i, 0)),
                  pl.BlockSpec((tm, pd), lambda i: (i, 0)),
                  pl.BlockSpec((1, d), lambda i: (0, 0)),
                  pl.BlockSpec((d, d), lambda i: (0, 0)),
                  pl.BlockSpec((pd, d), lambda i: (0, 0))],
        out_specs=pl.BlockSpec((tm, d), lambda i: (i, 0)),
        out_shape=jax.ShapeDtypeStruct(h.shape, F32),
        compiler_params=_params("parallel"), name="ple")(h, p, gate_norm, gate_w, proj)


def _expand_heads(v, n_heads):
    rows = v.shape[0]
    lane = lax.broadcasted_iota(jnp.int32, (rows, V7X_LANES), 1)
    per = V7X_LANES // SSD_HEAD_DIM
    chunks = []
    for c in range(n_heads // per):
        a = jnp.broadcast_to(v[:, per * c:per * c + 1], (rows, V7X_LANES))
        b = jnp.broadcast_to(v[:, per * c + 1:per * c + 2], (rows, V7X_LANES))
        chunks.append(jnp.where(lane < SSD_HEAD_DIM, a, b))
    return jnp.concatenate(chunks, axis=1)


def _ssd_kernel(xbc_ref, z_ref, dt_ref, cw_ref, cb_ref, dtb_ref, alog_ref, dskip_ref, gn_ref,
                y_ref, state_ref, halo_ref, *, inner, n_heads):
    ci = pl.program_id(1)
    cl = SSD_CHUNK
    gs = SSD_GROUPS * SSD_STATE
    hg = n_heads // SSD_GROUPS
    gw = inner // SSD_GROUPS

    @pl.when(ci == 0)
    def _():
        state_ref[...] = jnp.zeros_like(state_ref)
        halo_ref[...] = jnp.zeros_like(halo_ref)

    xraw = xbc_ref[...]
    halo = halo_ref[...]
    w = cw_ref[...]
    row8 = lax.broadcasted_iota(jnp.int32, halo.shape, 0)
    conv = xraw * w[CONV_W - 1:CONV_W, :] + cb_ref[...]
    for d in range(1, CONV_W):
        rolled = pltpu.roll(xraw, d, 0)
        top = jnp.where(row8 >= d, rolled[0:V7X_SUBLANES, :], pltpu.roll(halo, d, 0))
        shifted = jnp.concatenate([top, rolled[V7X_SUBLANES:, :]], axis=0)
        conv = conv + shifted * w[CONV_W - 1 - d:CONV_W - d, :]
    halo_ref[...] = xraw[cl - V7X_SUBLANES:cl, :]
    xc = _silu(conv)
    xs = xc[:, 0:inner]
    bm = xc[:, inner:inner + gs]
    cm = xc[:, inner + gs:inner + 2 * gs]

    dt = _softplus(dt_ref[...] + dtb_ref[...])
    da = dt * (-jnp.exp(alog_ref[...]))
    rowl = lax.broadcasted_iota(jnp.int32, da.shape, 0)
    cum = da
    d = 1
    while d < cl:
        cum = cum + _row_shift(cum, d, 0.0, rowl)
        d *= 2
    cum_t = cum.T
    dt_x = _expand_heads(dt, n_heads)
    cum_x = _expand_heads(cum, n_heads)
    last_x = cum_x[cl - 1:cl, :]
    xdt = xs * dt_x
    xdt_b = xdt.astype(BF16)
    tail_xdt = (jnp.exp(last_x - cum_x) * xdt).astype(BF16)
    exp_cum_x = jnp.exp(cum_x)
    state_decay = jnp.exp(last_x)

    tril = (lax.broadcasted_iota(jnp.int32, (cl, cl), 0) >= lax.broadcasted_iota(jnp.int32, (cl, cl), 1))
    lane = lax.broadcasted_iota(jnp.int32, (cl, V7X_LANES), 1)
    per = V7X_LANES // SSD_HEAD_DIM
    nt = (((1,), (1,)), ((), ()))
    dskip = dskip_ref[...]
    gn = gn_ref[...]
    for g in range(SSD_GROUPS):
        cg = cm[:, g * SSD_STATE:(g + 1) * SSD_STATE].astype(BF16)
        bg_f = bm[:, g * SSD_STATE:(g + 1) * SSD_STATE]
        cb = lax.dot_general(cg, bg_f.astype(BF16), nt, preferred_element_type=F32)
        st = state_ref[:, g * gw:(g + 1) * gw]
        y_off = jnp.dot(cg, st.astype(BF16), preferred_element_type=F32) * exp_cum_x[:, g * gw:(g + 1) * gw]
        diag = []
        for c in range(hg // per):
            pair = []
            for half in range(per):
                h = g * hg + c * per + half
                seg = cum[:, h:h + 1] - cum_t[h:h + 1, :]
                decay = jnp.exp(jnp.where(tril, seg, -jnp.inf))
                lo = g * gw + c * V7X_LANES
                pair.append(jnp.dot((cb * decay).astype(BF16), xdt_b[:, lo:lo + V7X_LANES],
                                    preferred_element_type=F32))
            diag.append(jnp.where(lane < SSD_HEAD_DIM, pair[0], pair[1]))
        y_diag = jnp.concatenate(diag, axis=1)
        new_st = st * state_decay[:, g * gw:(g + 1) * gw] + jnp.dot(
            bg_f.T.astype(BF16), tail_xdt[:, g * gw:(g + 1) * gw], preferred_element_type=F32)
        state_ref[:, g * gw:(g + 1) * gw] = new_st

        y = y_diag + y_off + dskip[:, g * gw:(g + 1) * gw] * xs[:, g * gw:(g + 1) * gw]
        y = y * _silu(z_ref[:, g * gw:(g + 1) * gw])
        y = y * lax.rsqrt(jnp.mean(y * y, axis=-1, keepdims=True) + NORM_EPS)
        y_ref[:, g * gw:(g + 1) * gw] = (y * gn[:, g * gw:(g + 1) * gw]).astype(y_ref.dtype)


def _ssd(xbc, z, dt, cw, cb, dtb, alog, dskip_x, gnorm, bsz, seq, inner, n_heads):
    nc = seq // SSD_CHUNK
    cl = SSD_CHUNK
    cch = xbc.shape[1]

    def xmap(b, c):
        return (b * nc + c, 0)

    def pmap(b, c):
        return (0, 0)

    return pl.pallas_call(
        functools.partial(_ssd_kernel, inner=inner, n_heads=n_heads), grid=(bsz, nc),
        in_specs=[pl.BlockSpec((cl, cch), xmap), pl.BlockSpec((cl, inner), xmap),
                  pl.BlockSpec((cl, V7X_LANES), xmap),
                  pl.BlockSpec((CONV_W, cch), pmap), pl.BlockSpec((1, cch), pmap),
                  pl.BlockSpec((1, V7X_LANES), pmap), pl.BlockSpec((1, V7X_LANES), pmap),
                  pl.BlockSpec((1, inner), pmap), pl.BlockSpec((1, inner), pmap)],
        out_specs=pl.BlockSpec((cl, inner), xmap),
        out_shape=jax.ShapeDtypeStruct((bsz * seq, inner), BF16),
        scratch_shapes=[pltpu.VMEM((SSD_STATE, inner), F32), pltpu.VMEM((V7X_SUBLANES, cch), F32)],
        compiler_params=_params("parallel", "arbitrary"), name="ssd")(
            xbc, z, dt, cw, cb, dtb, alog, dskip_x, gnorm)


def _router_kernel(h_ref, g_ref, r_ref, gates_ref):
    xn = _rms(h_ref[...], g_ref[...])
    logits = jnp.dot(xn, r_ref[...], preferred_element_type=F32, precision=lax.Precision.HIGHEST)
    lane = lax.broadcasted_iota(jnp.int32, logits.shape, 1)
    logits = jnp.where(lane < N_EXPERTS, logits, -jnp.inf)
    m1 = jnp.max(logits, axis=-1, keepdims=True)
    i1 = jnp.min(jnp.where(logits == m1, lane, V7X_LANES), axis=-1, keepdims=True)
    rest = jnp.where(lane == i1, -jnp.inf, logits)
    m2 = jnp.max(rest, axis=-1, keepdims=True)
    i2 = jnp.min(jnp.where(rest == m2, lane, V7X_LANES), axis=-1, keepdims=True)
    e2 = jnp.exp(m2 - m1)
    w1 = 1.0 / (1.0 + e2)
    w2 = e2 / (1.0 + e2)
    gates_ref[...] = jnp.where(lane == 0, i1.astype(F32), jnp.where(
        lane == 1, i2.astype(F32), jnp.where(lane == 2, w1, jnp.where(lane == 3, w2, 0.0))))


def _router(h, gain, router_p, tm):
    m, d = h.shape
    return pl.pallas_call(
        _router_kernel, grid=(m // tm,),
        in_specs=[pl.BlockSpec((tm, d), lambda i: (i, 0)),
                  pl.BlockSpec((1, d), lambda i: (0, 0)),
                  pl.BlockSpec((d, V7X_LANES), lambda i: (0, 0))],
        out_specs=pl.BlockSpec((tm, V7X_LANES), lambda i: (i, 0)),
        out_shape=jax.ShapeDtypeStruct((m, V7X_LANES), F32),
        compiler_params=_params("parallel"), name="router")(h, gain, router_p)


DMA_UNROLL = 8


def _row_dma_loop(n_rows, body):
    def trip(r, carry):
        body(r)
        return carry

    lax.fori_loop(0, n_rows, trip, 0, unroll=DMA_UNROLL)


def _moe_group_kernel(texp_ref, nused_ref, tok_ref, h_hbm, g_ref, wg_ref, wu_ref, wd_ref, o_ref,
                      xbuf, xn_ref, acc_ref, sems):
    t = pl.program_id(0)
    k = pl.program_id(1)
    n_used = nused_ref[0]
    active = t < n_used
    last = k == pl.num_programs(1) - 1
    n_rows = tok_ref.shape[1]
    slot = lax.rem(t, 2)

    def copy(tile, r, slot):
        return pltpu.make_async_copy(h_hbm.at[pl.ds(tok_ref[tile, r], 1), :],
                                     xbuf.at[slot, pl.ds(r, 1), :], sems.at[slot])

    def fetch(tile, slot):
        _row_dma_loop(n_rows, lambda r: copy(tile, r, slot).start())

    @pl.when(jnp.logical_and(active, k == 0))
    def _():
        @pl.when(t == 0)
        def _():
            fetch(t, slot)

        @pl.when(t + 1 < n_used)
        def _():
            fetch(t + 1, 1 - slot)

        _row_dma_loop(n_rows, lambda r: copy(t, r, slot).wait())
        xn_ref[...] = _rms(xbuf[slot], g_ref[...]).astype(BF16)
        acc_ref[...] = jnp.zeros_like(acc_ref)

    @pl.when(active)
    def _():
        xn = xn_ref[...]
        gate = jnp.dot(xn, wg_ref[...], preferred_element_type=F32)
        up = jnp.dot(xn, wu_ref[...], preferred_element_type=F32)
        act = (_silu(gate) * up).astype(BF16)
        acc_ref[...] += jnp.dot(act, wd_ref[...], preferred_element_type=F32)

    @pl.when(jnp.logical_and(active, last))
    def _():
        o_ref[...] = acc_ref[...]

    @pl.when(jnp.logical_and(jnp.logical_not(active), last))
    def _():
        o_ref[...] = jnp.zeros_like(o_ref)


def _moe_group(h, gain, tok_tiles, tile_expert, n_used, wg, wu, wd, tf):
    m, d = h.shape
    ne, _, ff = wg.shape
    n_tiles, tmg = tok_tiles.shape
    nk = ff // tf

    def wmap_in(t, k, texp, nused, tok):
        return (texp[t], 0, jnp.where(t < nused[0], k, nk - 1))

    def wmap_out(t, k, texp, nused, tok):
        return (texp[t], jnp.where(t < nused[0], k, nk - 1), 0)

    def tmap(t, k, texp, nused, tok):
        return (t, 0)

    def cmap(t, k, texp, nused, tok):
        return (0, 0)

    grid_spec = pltpu.PrefetchScalarGridSpec(
        num_scalar_prefetch=3, grid=(n_tiles, nk),
        in_specs=[pl.BlockSpec(memory_space=pl.ANY), pl.BlockSpec((1, d), cmap),
                  pl.BlockSpec((None, d, tf), wmap_in), pl.BlockSpec((None, d, tf), wmap_in),
                  pl.BlockSpec((None, tf, d), wmap_out)],
        out_specs=pl.BlockSpec((tmg, d), tmap),
        scratch_shapes=[pltpu.VMEM((2, tmg, d), F32), pltpu.VMEM((tmg, d), BF16), pltpu.VMEM((tmg, d), F32),
                        pltpu.SemaphoreType.DMA((2,))])
    return pl.pallas_call(
        _moe_group_kernel, grid_spec=grid_spec,
        out_shape=jax.ShapeDtypeStruct((n_tiles * tmg, d), F32),
        compiler_params=_params("arbitrary", "arbitrary"), name="moe_group")(
            tile_expert, n_used, tok_tiles, h, gain, wg, wu, wd)


def _moe_combine_ple_kernel(pos_ref, ys_hbm, h_ref, sel_ref, p_ref, gn_ref, gw_ref, proj_ref, fn_ref, o_ref,
                            ybuf, sems):
    i = pl.program_id(0)
    n_rows = pos_ref.shape[1]
    tm = n_rows // 2

    def copy(tile, r, slot):
        return pltpu.make_async_copy(ys_hbm.at[pl.ds(pos_ref[tile, r], 1), :],
                                     ybuf.at[slot, pl.ds(r, 1), :], sems.at[slot])

    def fetch(tile, slot):
        _row_dma_loop(n_rows, lambda r: copy(tile, r, slot).start())

    slot = lax.rem(i, 2)

    @pl.when(i == 0)
    def _():
        fetch(i, slot)

    @pl.when(i + 1 < pl.num_programs(0))
    def _():
        fetch(i + 1, 1 - slot)

    _row_dma_loop(n_rows, lambda r: copy(i, r, slot).wait())
    sel = sel_ref[...]
    y = ybuf[slot]
    h = h_ref[...] + (sel[:, 2:3] * y[0:tm, :] + sel[:, 3:4] * y[tm:n_rows, :])
    o_ref[...] = _rms(_ple_add(h, p_ref, gn_ref, gw_ref, proj_ref), fn_ref[...])


def _moe_combine_ple(h, ys, pos_tiles, sel, p, gate_norm, gate_w, proj, final_norm):
    m, d = h.shape
    pd = p.shape[1]
    n_tiles, n_rows = pos_tiles.shape
    tm = n_rows // 2

    def tmap(i, pos):
        return (i, 0)

    def cmap(i, pos):
        return (0, 0)

    grid_spec = pltpu.PrefetchScalarGridSpec(
        num_scalar_prefetch=1, grid=(n_tiles,),
        in_specs=[pl.BlockSpec(memory_space=pl.ANY),
                  pl.BlockSpec((tm, d), tmap), pl.BlockSpec((tm, V7X_LANES), tmap), pl.BlockSpec((tm, pd), tmap),
                  pl.BlockSpec((1, d), cmap), pl.BlockSpec((d, d), cmap),
                  pl.BlockSpec((pd, d), cmap), pl.BlockSpec((1, d), cmap)],
        out_specs=pl.BlockSpec((tm, d), tmap),
        scratch_shapes=[pltpu.VMEM((2, n_rows, d), F32), pltpu.SemaphoreType.DMA((2,))])
    return pl.pallas_call(
        _moe_combine_ple_kernel, grid_spec=grid_spec,
        out_shape=jax.ShapeDtypeStruct(h.shape, F32),
        compiler_params=_params("arbitrary"), name="moe_combine_ple")(
            pos_tiles, ys, h, sel, p, gate_norm, gate_w, proj, final_norm)


def _moe_dispatch_plan(sel, tmg, tmc):
    m = sel.shape[0]
    e_flat = sel[:, 0:2].astype(jnp.int32).reshape(-1)
    onehot = (e_flat[:, None] == jnp.arange(N_EXPERTS, dtype=jnp.int32)[None, :]).astype(jnp.int32)
    csum = jnp.cumsum(onehot, axis=0)
    rank = jnp.sum((csum - onehot) * onehot, axis=1)
    tiles_e = (csum[-1] + tmg - 1) // tmg
    tile_end = jnp.cumsum(tiles_e)
    pos = (tile_end - tiles_e)[e_flat] * tmg + rank
    n_tiles = (2 * m) // tmg + N_EXPERTS
    tile_expert = jnp.minimum(jnp.searchsorted(tile_end, jnp.arange(n_tiles, dtype=jnp.int32), side="right"),
                              N_EXPERTS - 1).astype(jnp.int32)
    tok_tiles = jnp.zeros((n_tiles * tmg,), jnp.int32).at[pos].set(jnp.arange(2 * m, dtype=jnp.int32) // 2)
    pos_tiles = pos.reshape(m // tmc, tmc, 2).transpose(0, 2, 1).reshape(m // tmc, 2 * tmc)
    return (pos_tiles.astype(jnp.int32), tok_tiles.reshape(n_tiles, tmg), tile_expert,
            tile_end[-1:].astype(jnp.int32))


def _row(v):
    return v.reshape(1, -1).astype(F32)


def _pad_cols(w, width):
    return jnp.pad(w, ((0, 0), (0, width - w.shape[1])))


def _tile(m, want):
    t = min(want, m)
    while m % t:
        t //= 2
    return t


def _even_layer(h, p_i, bsz, seq, mix_norm, w_in, conv_w, conv_b, wa, ba, wx, bx, lam,
                w_out, ffn_norm, ffn_wg, ffn_wu, ffn_wd, ple_proj, ple_gate_norm, ple_gate):
    m = h.shape[0]
    qw = A_HEADS * HEAD_DIM
    kw = A_KV_HEADS * HEAD_DIM
    iw = IDX_HEADS * IDX_DIM
    o = 0
    wq = w_in[:, o:o + qw] * HEAD_DIM ** -0.5; o += qw
    wk = w_in[:, o:o + kw]; o += kw
    wv = w_in[:, o:o + kw]; o += kw
    wqi = w_in[:, o:o + iw]; o += iw
    wki = w_in[:, o:o + IDX_DIM]; o += IDX_DIM
    wwi = w_in[:, o:o + IDX_HEADS]; o += IDX_HEADS
    wxg = w_in[:, o:o + LRU_WIDTH]; o += LRU_WIDTH
    wxr = w_in[:, o:o + LRU_WIDTH]

    def dup_heads(wm):
        parts = []
        for g in range(wm.shape[1] // HEAD_DIM):
            blk = wm[:, g * HEAD_DIM:(g + 1) * HEAD_DIM]
            parts += [blk, blk]
        return jnp.concatenate(parts, axis=1)

    w_all = jnp.concatenate([wq, dup_heads(wk), dup_heads(wv), wqi, dup_heads(wki),
                             _pad_cols(wwi, V7X_LANES), wxg, wxr], axis=1).astype(BF16)
    sections = ((qw, True), (2 * kw, True), (2 * kw, False), (iw, True), (2 * IDX_DIM, True),
                (V7X_LANES, False), (LRU_WIDTH, False), (LRU_WIDTH, False))
    dtypes = (BF16, BF16, BF16, BF16, BF16, F32, F32, F32)
    tm = _tile(seq, 512)
    q, kk, vv, qi, kiki, wi, xg, xr = _in_proj(h, _row(mix_norm), w_all, sections, dtypes, tm,
                                               rope_tables=_rope_tables(seq), seq=seq)
    o_a = _dsa(q, qi, wi, kk, vv, kiki, bsz, seq)
    o_b = _lru(xr, xg, conv_w, _row(conv_b), _block_diag_pairs(wa), _row(ba),
               _block_diag_pairs(wx), _row(bx), _row(lam), bsz, seq)
    w_out_b = w_out.astype(BF16)
    h = _res_matmul(h, [o_a, o_b], [w_out_b[:qw], w_out_b[qw:]], _tile(m, 512))
    h = _ffn(h, _row(ffn_norm), ffn_wg.astype(BF16), ffn_wu.astype(BF16), ffn_wd.astype(BF16),
             _tile(m, 1024), 512)
    return _ple(h, p_i, _row(ple_gate_norm), ple_gate.astype(BF16), ple_proj.astype(BF16), _tile(m, 512))


def _odd_layer(h, p_i, bsz, seq, mix_norm, w_in, conv_w, conv_b, dt_bias, a_log, d_skip, gnorm, w_out,
               moe_norm, router, exp_wg, exp_wu, exp_wd, ple_proj, ple_gate_norm, ple_gate, final_norm):
    m = h.shape[0]
    inner = gnorm.shape[0]
    n_heads = dt_bias.shape[0]
    cch = conv_w.shape[1]
    w_z = w_in[:, :inner]
    w_xbc = w_in[:, inner:inner + cch]
    w_dt = _pad_cols(w_in[:, inner + cch:], V7X_LANES)
    w_all = jnp.concatenate([w_z, w_xbc, w_dt], axis=1).astype(BF16)
    sections = ((inner, False), (cch, False), (V7X_LANES, False))
    z, xbc, dt = _in_proj(h, _row(mix_norm), w_all, sections, (F32, F32, F32), _tile(m, 256))
    pad = V7X_LANES - n_heads
    y = _ssd(xbc, z, dt, conv_w, _row(conv_b), _row(jnp.pad(dt_bias, (0, pad))),
             _row(jnp.pad(a_log, (0, pad))), _row(jnp.repeat(d_skip, SSD_HEAD_DIM)), _row(gnorm),
             bsz, seq, inner, n_heads)
    h = _res_matmul(h, [y], [w_out.astype(BF16)], _tile(m, 512))
    sel = _router(h, _row(moe_norm), _pad_cols(router, V7X_LANES), _tile(m, 512))
    tmg = _tile(2 * m, 512)
    tmc = _tile(m, 256)
    pos_tiles, tok_tiles, tile_expert, n_used = _moe_dispatch_plan(sel, tmg, tmc)
    ys = _moe_group(h, _row(moe_norm), tok_tiles, tile_expert, n_used,
                    exp_wg.astype(BF16), exp_wu.astype(BF16), exp_wd.astype(BF16), 896)
    return _moe_combine_ple(h, ys, pos_tiles, sel, p_i, _row(ple_gate_norm), ple_gate.astype(BF16),
                            ple_proj.astype(BF16), _row(final_norm))


def kernel(x, p, e_mix_norm, e_w_in, e_lru_conv_w, e_lru_conv_b, e_lru_wa, e_lru_ba, e_lru_wx, e_lru_bx, e_lru_lambda, e_w_out, e_ffn_norm, e_ffn_wg, e_ffn_wu, e_ffn_wd, e_ple_proj, e_ple_gate_norm, e_ple_gate, o_mix_norm, o_w_in, o_conv_w, o_conv_b, o_dt_bias, o_a_log, o_d_skip, o_gnorm, o_w_out, o_moe_norm, o_router, o_exp_wg, o_exp_wu, o_exp_wd, o_ple_proj, o_ple_gate_norm, o_ple_gate, final_norm):
    bsz, seq, d = x.shape
    m = bsz * seq
    h = x.reshape(m, d)
    pf = p.reshape(p.shape[0], m, p.shape[-1])
    h = _even_layer(h, pf[0], bsz, seq, e_mix_norm, e_w_in, e_lru_conv_w, e_lru_conv_b, e_lru_wa, e_lru_ba,
                    e_lru_wx, e_lru_bx, e_lru_lambda, e_w_out, e_ffn_norm, e_ffn_wg, e_ffn_wu, e_ffn_wd,
                    e_ple_proj, e_ple_gate_norm, e_ple_gate)
    h = _odd_layer(h, pf[1], bsz, seq, o_mix_norm, o_w_in, o_conv_w, o_conv_b, o_dt_bias, o_a_log, o_d_skip,
                   o_gnorm, o_w_out, o_moe_norm, o_router, o_exp_wg, o_exp_wu, o_exp_wd,
                   o_ple_proj, o_ple_gate_norm, o_ple_gate, final_norm)
    return h.reshape(bsz, seq, d)
```
